```python
import math
import numpy as np
import jax
import jax.numpy as jnp
from jax import lax

D_MODEL = 1024
BATCH = 4
SEQ = 4096
DEPTH = 4

HEAD_DIM = 64
ROPE_THETA = 10000.0
Q_BLOCK = 128
SWA_Q_HEADS = 8
SWA_KV_HEADS = 2
SWA_WINDOW = 128
NSA_Q_HEADS = 8
NSA_KV_HEADS = 2
CMP_LEN = 32
CMP_STRIDE = 16
CMP_HIDDEN = 128
SEL_LEN = 64
SEL_TOPK = 16
NSA_WINDOW = 512
N_BRANCH = 3
ATTN_IN = (SWA_Q_HEADS + 2 * SWA_KV_HEADS) * HEAD_DIM + (NSA_Q_HEADS + 6 * NSA_KV_HEADS) * HEAD_DIM + N_BRANCH * NSA_Q_HEADS
ATTN_OUT = (SWA_Q_HEADS + NSA_Q_HEADS) * HEAD_DIM
SSM_EXPAND = 2
D_INNER = SSM_EXPAND * D_MODEL
SSM_HEAD_DIM = 64
SSM_HEADS = D_INNER // SSM_HEAD_DIM
SSM_GROUPS = 8
D_STATE = 128
CONV_W = 4
SSD_CHUNK = 128
CONV_CH = D_INNER + 2 * SSM_GROUPS * D_STATE
SSM_IN = D_INNER + CONV_CH + SSM_HEADS
D_FF = 2816
N_EVEN = (DEPTH + 1) // 2
N_ODD = DEPTH // 2
RMS_EPS = 1e-6
NEG_INF = -1e30
SEL_FORCE = 1e4

kernel_name = 'hybrid_swa_nsa_ssd_macaron'


def rmsnorm(x, g):
    xf = x.astype(jnp.float32)
    y = xf * lax.rsqrt(jnp.mean(xf * xf, axis=-1, keepdims=True) + RMS_EPS)
    return (y * g.astype(jnp.float32)).astype(x.dtype)


def swiglu(h, w_gate, w_up, w_down):
    return (jax.nn.silu(h @ w_gate) * (h @ w_up)) @ w_down


def rope_tables(seq):
    inv = 1.0 / (ROPE_THETA ** (jnp.arange(0, HEAD_DIM, 2, dtype=jnp.float32) / HEAD_DIM))
    ang = jnp.arange(seq, dtype=jnp.float32)[:, None] * inv[None, :]
    return jnp.cos(ang), jnp.sin(ang)


def apply_rope(t, cos, sin):
    t1, t2 = jnp.split(t.astype(jnp.float32), 2, axis=-1)
    c = cos[None, :, None, :]
    s = sin[None, :, None, :]
    return jnp.concatenate([t1 * c - t2 * s, t2 * c + t1 * s], axis=-1).astype(t.dtype)


def masked_softmax(s, mask, sink=None):
    s = jnp.where(mask, s, NEG_INF)
    m = jnp.max(s, axis=-1, keepdims=True)
    if sink is not None:
        m = jnp.maximum(m, sink)
    e = jnp.where(mask, jnp.exp(s - m), 0.0)
    den = jnp.sum(e, axis=-1, keepdims=True)
    if sink is not None:
        den = den + jnp.exp(sink - m)
    return e / jnp.maximum(den, 1e-30)


def banded_attention(q, k, v, window, sinks=None):
    bsz, g, r, seq, dh = q.shape
    nq = seq // Q_BLOCK
    nb = -(-window // Q_BLOCK)
    qb = q.reshape(bsz, g, r, nq, Q_BLOCK, dh)
    pad = ((0, 0), (0, 0), (nb * Q_BLOCK, 0), (0, 0))
    kp = jnp.pad(k, pad).reshape(bsz, g, nq + nb, Q_BLOCK, dh)
    vp = jnp.pad(v, pad).reshape(bsz, g, nq + nb, Q_BLOCK, dh)
    kband = jnp.concatenate([kp[:, :, j:j + nq] for j in range(nb + 1)], axis=3)
    vband = jnp.concatenate([vp[:, :, j:j + nq] for j in range(nb + 1)], axis=3)
    qpos = jnp.arange(nq)[:, None] * Q_BLOCK + jnp.arange(Q_BLOCK)[None, :]
    kpos = (jnp.arange(nq)[:, None] - nb) * Q_BLOCK + jnp.arange((nb + 1) * Q_BLOCK)[None, :]
    diff = qpos[:, :, None] - kpos[:, None, :]
    mask = (diff >= 0) & (diff < window) & (kpos[:, None, :] >= 0)
    s = jnp.einsum('bgrnid,bgnkd->bgrnik', qb, kband).astype(jnp.float32) * (dh ** -0.5)
    sink = None if sinks is None else sinks.astype(jnp.float32).reshape(1, g, r, 1, 1, 1)
    p = masked_softmax(s, mask, sink)
    o = jnp.einsum('bgrnik,bgnkd->bgrnid', p.astype(vband.dtype), vband)
    return o.reshape(bsz, g, r, seq, dh)


def compress(t, pos_emb, w1, w2):
    seq = t.shape[2]
    n_cmp = (seq - CMP_LEN) // CMP_STRIDE + 1
    idx = np.arange(n_cmp)[:, None] * CMP_STRIDE + np.arange(CMP_LEN)[None, :]
    blocks = t[:, :, idx] + pos_emb
    flat = blocks.reshape(blocks.shape[0], blocks.shape[1], n_cmp, CMP_LEN * HEAD_DIM)
    return jax.nn.gelu(flat @ w1) @ w2


def cmp_to_sel_weights(seq):
    n_cmp = (seq - CMP_LEN) // CMP_STRIDE + 1
    n_blk = seq // SEL_LEN
    cs = np.arange(n_cmp) * CMP_STRIDE
    ss = np.arange(n_blk) * SEL_LEN
    ov = np.minimum(cs[:, None] + CMP_LEN, ss[None, :] + SEL_LEN) - np.maximum(cs[:, None], ss[None, :])
    return (np.clip(ov, 0, None) / CMP_LEN).astype(np.float32)


def selected_attention(q, k, v, idx):
    bsz, g, r, seq, dh = q.shape
    n_blk = seq // SEL_LEN
    nq = seq // Q_BLOCK
    kb = k.reshape(bsz, g, n_blk, SEL_LEN, dh)
    vb = v.reshape(bsz, g, n_blk, SEL_LEN, dh)
    qs = jnp.moveaxis(q.reshape(bsz, g, r, nq, Q_BLOCK, dh), 3, 0)
    ids = jnp.moveaxis(idx.reshape(bsz, g, nq, Q_BLOCK, idx.shape[-1]), 2, 0)
    qpos = jnp.arange(seq).reshape(nq, Q_BLOCK)
    take = jax.vmap(jax.vmap(lambda blocks, i: blocks[i]))

    def one_block(args):
        qq, ii, pos = args
        kg = take(kb, ii)
        vg = take(vb, ii)
        s = jnp.einsum('bgrqd,bgqnkd->bgrqnk', qq, kg).astype(jnp.float32) * (dh ** -0.5)
        kpos = ii[..., None] * SEL_LEN + jnp.arange(SEL_LEN)
        mask = (kpos <= pos[:, None, None])[:, :, None]
        sh = s.shape
        p = masked_softmax(s.reshape(sh[0], sh[1], sh[2], sh[3], -1),
                           mask.reshape(sh[0], sh[1], 1, sh[3], -1)).reshape(sh)
        return jnp.einsum('bgrqnk,bgqnkd->bgrqd', p.astype(vg.dtype), vg)

    o = lax.map(one_block, (qs, ids, qpos))
    return jnp.moveaxis(o, 0, 3).reshape(bsz, g, r, seq, dh)


def nsa_attention(q, kc, vc, ks, vs, kw, vw, gates, ckp, ckw1, ckw2, cvp, cvw1, cvw2):
    seq = q.shape[3]
    kcmp = compress(kc, ckp, ckw1, ckw2)
    vcmp = compress(vc, cvp, cvw1, cvw2)
    n_cmp = kcmp.shape[2]
    t = jnp.arange(seq)
    cmp_end = jnp.arange(n_cmp) * CMP_STRIDE + CMP_LEN - 1
    s = jnp.einsum('bgrtd,bgcd->bgrtc', q, kcmp).astype(jnp.float32) * (HEAD_DIM ** -0.5)
    p_cmp = masked_softmax(s, cmp_end[None, :] <= t[:, None])
    o_cmp = jnp.einsum('bgrtc,bgcd->bgrtd', p_cmp.astype(vcmp.dtype), vcmp)
    n_blk = seq // SEL_LEN
    imp = jnp.einsum('bgrtc,cj->bgtj', p_cmp, jnp.asarray(cmp_to_sel_weights(seq)))
    cur = (t // SEL_LEN)[:, None]
    j = jnp.arange(n_blk)[None, :]
    valid = j <= cur
    forced = valid & ((j == 0) | (j == cur) | (j == cur - 1))
    score = imp + jnp.where(forced, SEL_FORCE, 0.0) - jnp.where(valid, 0.0, SEL_FORCE)
    _, idx = lax.top_k(score, min(SEL_TOPK, n_blk))
    o_slc = selected_attention(q, ks, vs, idx)
    o_win = banded_attention(q, kw, vw, NSA_WINDOW)
    o = gates[..., 0:1] * o_cmp + gates[..., 1:2] * o_slc + gates[..., 2:3] * o_win
    return o.astype(q.dtype)


def attn_mixer(h, w_in, w_out, sinks, ckp, ckw1, ckw2, cvp, cvw1, cvw2, cos, sin):
    bsz, seq, _ = h.shape
    dq_a = SWA_Q_HEADS * HEAD_DIM
    dkv_a = SWA_KV_HEADS * HEAD_DIM
    dq_b = NSA_Q_HEADS * HEAD_DIM
    dkv_b = NSA_KV_HEADS * HEAD_DIM
    sizes = [dq_a, dkv_a, dkv_a, dq_b] + [dkv_b] * 6
    cuts = [int(c) for c in np.cumsum(sizes)]
    qa, ka, va, qb, kc, vc, ks, vs, kw, vw, gt = jnp.split(h @ w_in, cuts, axis=-1)

    def heads(t):
        return t.reshape(bsz, seq, -1, HEAD_DIM)

    def q_layout(t, n_groups):
        return t.reshape(bsz, seq, n_groups, -1, HEAD_DIM).transpose(0, 2, 3, 1, 4)

    def kv_layout(t):
        return t.transpose(0, 2, 1, 3)

    def rot(t):
        return apply_rope(heads(t), cos, sin)

    o_a = banded_attention(q_layout(rot(qa), SWA_KV_HEADS), kv_layout(rot(ka)), kv_layout(heads(va)),
                           SWA_WINDOW, sinks)
    gates = jax.nn.sigmoid(gt.reshape(bsz, seq, NSA_KV_HEADS, -1, N_BRANCH).transpose(0, 2, 3, 1, 4))
    o_b = nsa_attention(q_layout(rot(qb), NSA_KV_HEADS),
                        kv_layout(rot(kc)), kv_layout(heads(vc)),
                        kv_layout(rot(ks)), kv_layout(heads(vs)),
                        kv_layout(rot(kw)), kv_layout(heads(vw)),
                        gates, ckp, ckw1, ckw2, cvp, cvw1, cvw2)

    def merge(o):
        return o.transpose(0, 3, 1, 2, 4).reshape(bsz, seq, -1)

    return jnp.concatenate([merge(o_a), merge(o_b)], axis=-1) @ w_out


def ssd_scan(x, dt, a, bm, cm):
    bsz, seq, nh, hp = x.shape
    g, n = bm.shape[2], bm.shape[3]
    r = nh // g
    L = SSD_CHUNK
    nc = seq // L
    x = x.astype(jnp.float32)
    xdt = (x * dt[..., None]).reshape(bsz, nc, L, g, r, hp)
    da = (dt * a).reshape(bsz, nc, L, g, r).transpose(0, 3, 4, 1, 2)
    bc = bm.astype(jnp.float32).reshape(bsz, nc, L, g, n)
    cc = cm.astype(jnp.float32).reshape(bsz, nc, L, g, n)
    a_cs = jnp.cumsum(da, axis=-1)
    causal = jnp.tril(jnp.ones((L, L), dtype=bool))
    seg = a_cs[..., :, None] - a_cs[..., None, :]
    lmat = jnp.exp(jnp.where(causal, seg, -jnp.inf))
    cb = jnp.einsum('bclgn,bcsgn->bgcls', cc, bc)
    y_diag = jnp.einsum('bgrcls,bcsgrp->bclgrp', cb[:, :, None] * lmat, xdt)
    decay = jnp.exp(a_cs[..., -1:] - a_cs)
    states = jnp.einsum('bclgn,bgrcl,bclgrp->cbgrpn', bc, decay, xdt)
    chunk_decay = jnp.moveaxis(jnp.exp(a_cs[..., -1]), -1, 0)

    def step(hc, inp):
        s_c, d_c = inp
        return d_c[..., None, None] * hc + s_c, hc

    h0 = jnp.zeros((bsz, g, r, hp, n), jnp.float32)
    _, prev = lax.scan(step, h0, (states, chunk_decay))
    y_off = jnp.einsum('bclgn,cbgrpn,bgrcl->bclgrp', cc, prev, jnp.exp(a_cs))
    return (y_diag + y_off).reshape(bsz, seq, nh, hp)


def gated_rmsnorm(y, z, w):
    gy = y.astype(jnp.float32) * jax.nn.silu(z.astype(jnp.float32))
    gg = gy.reshape(gy.shape[:-1] + (SSM_GROUPS, -1))
    gg = gg * lax.rsqrt(jnp.mean(gg * gg, axis=-1, keepdims=True) + RMS_EPS)
    return (gg.reshape(gy.shape) * w.astype(jnp.float32)).astype(z.dtype)


def mamba2_mixer(h, w_in, conv_w, conv_b, dt_bias, a_log, d_skip, norm_w, w_out):
    bsz, seq, _ = h.shape
    z, xbc, dt = jnp.split(h @ w_in, [D_INNER, D_INNER + CONV_CH], axis=-1)
    xbc = lax.conv_general_dilated(xbc, conv_w[:, None, :], window_strides=(1,),
                                   padding=[(CONV_W - 1, 0)],
                                   dimension_numbers=('NWC', 'WIO', 'NWC'),
                                   feature_group_count=CONV_CH) + conv_b
    xbc = jax.nn.silu(xbc)
    xs, bm, cm = jnp.split(xbc, [D_INNER, D_INNER + SSM_GROUPS * D_STATE], axis=-1)
    dt = jax.nn.softplus(dt.astype(jnp.float32) + dt_bias.astype(jnp.float32))
    a = -jnp.exp(a_log.astype(jnp.float32))
    xh = xs.reshape(bsz, seq, SSM_HEADS, SSM_HEAD_DIM)
    y = ssd_scan(xh, dt, a,
                 bm.reshape(bsz, seq, SSM_GROUPS, D_STATE),
                 cm.reshape(bsz, seq, SSM_GROUPS, D_STATE))
    y = y + d_skip.astype(jnp.float32)[:, None] * xh.astype(jnp.float32)
    y = gated_rmsnorm(y.reshape(bsz, seq, D_INNER), z, norm_w)
    return y @ w_out


def setup_inputs(seed: int = 0) -> dict:
    key = jax.random.key(seed)
    keys = iter(jax.random.split(key, 32))
    ne, no = N_EVEN, N_ODD

    def nrm(shape, scale):
        return jax.random.normal(next(keys), shape, jnp.float32) * scale

    x = nrm((BATCH, SEQ, D_MODEL), 1.0)
    norm_gains = 1.0 + nrm((DEPTH, 3, D_MODEL), 0.02)
    final_norm = 1.0 + nrm((D_MODEL,), 0.02)
    ffn_w_gate = nrm((DEPTH, 2, D_MODEL, D_FF), D_MODEL ** -0.5)
    ffn_w_up = nrm((DEPTH, 2, D_MODEL, D_FF), D_MODEL ** -0.5)
    ffn_w_down = nrm((DEPTH, 2, D_FF, D_MODEL), D_FF ** -0.5)
    attn_w_in = nrm((ne, D_MODEL, ATTN_IN), D_MODEL ** -0.5)
    attn_w_out = nrm((ne, ATTN_OUT, D_MODEL), ATTN_OUT ** -0.5)
    attn_sinks = nrm((ne, SWA_Q_HEADS), 1.0)
    cmp_k_pos = nrm((ne, CMP_LEN, HEAD_DIM), 0.1)
    cmp_k_w1 = nrm((ne, CMP_LEN * HEAD_DIM, CMP_HIDDEN), (CMP_LEN * HEAD_DIM) ** -0.5)
    cmp_k_w2 = nrm((ne, CMP_HIDDEN, HEAD_DIM), CMP_HIDDEN ** -0.5)
    cmp_v_pos = nrm((ne, CMP_LEN, HEAD_DIM), 0.1)
    cmp_v_w1 = nrm((ne, CMP_LEN * HEAD_DIM, CMP_HIDDEN), (CMP_LEN * HEAD_DIM) ** -0.5)
    cmp_v_w2 = nrm((ne, CMP_HIDDEN, HEAD_DIM), CMP_HIDDEN ** -0.5)
    ssm_w_in = nrm((no, D_MODEL, SSM_IN), D_MODEL ** -0.5)
    ssm_conv_w = nrm((no, CONV_W, CONV_CH), CONV_W ** -0.5)
    ssm_conv_b = nrm((no, CONV_CH), 0.02)
    dt0 = jnp.exp(jax.random.uniform(next(keys), (no, SSM_HEADS), jnp.float32,
                                     minval=math.log(1e-3), maxval=math.log(1e-1)))
    ssm_dt_bias = dt0 + jnp.log(-jnp.expm1(-dt0))
    ssm_a_log = jnp.log(jax.random.uniform(next(keys), (no, SSM_HEADS), jnp.float32, minval=1.0, maxval=16.0))
    ssm_d = 1.0 + nrm((no, SSM_HEADS), 0.1)
    ssm_norm = 1.0 + nrm((no, D_INNER), 0.02)
    ssm_w_out = nrm((no, D_INNER, D_MODEL), D_INNER ** -0.5)
    return {'x': x, 'norm_gains': norm_gains, 'final_norm': final_norm,
            'ffn_w_gate': ffn_w_gate, 'ffn_w_up': ffn_w_up, 'ffn_w_down': ffn_w_down,
            'attn_w_in': attn_w_in, 'attn_w_out': attn_w_out, 'attn_sinks': attn_sinks,
            'cmp_k_pos': cmp_k_pos, 'cmp_k_w1': cmp_k_w1, 'cmp_k_w2': cmp_k_w2,
            'cmp_v_pos': cmp_v_pos, 'cmp_v_w1': cmp_v_w1, 'cmp_v_w2': cmp_v_w2,
            'ssm_w_in': ssm_w_in, 'ssm_conv_w': ssm_conv_w, 'ssm_conv_b': ssm_conv_b,
            'ssm_dt_bias': ssm_dt_bias, 'ssm_a_log': ssm_a_log, 'ssm_d': ssm_d,
            'ssm_norm': ssm_norm, 'ssm_w_out': ssm_w_out}


def reference(x, norm_gains, final_norm, ffn_w_gate, ffn_w_up, ffn_w_down,
              attn_w_in, attn_w_out, attn_sinks,
              cmp_k_pos, cmp_k_w1, cmp_k_w2, cmp_v_pos, cmp_v_w1, cmp_v_w2,
              ssm_w_in, ssm_conv_w, ssm_conv_b, ssm_dt_bias, ssm_a_log, ssm_d,
              ssm_norm, ssm_w_out):
    cos, sin = rope_tables(x.shape[1])
    for i in range(DEPTH):
        g = norm_gains[i]
        x = x + 0.5 * swiglu(rmsnorm(x, g[0]), ffn_w_gate[i, 0], ffn_w_up[i, 0], ffn_w_down[i, 0])
        h = rmsnorm(x, g[1])
        j = i // 2
        if i % 2 == 0:
            x = x + attn_mixer(h, attn_w_in[j], attn_w_out[j], attn_sinks[j],
                               cmp_k_pos[j], cmp_k_w1[j], cmp_k_w2[j],
                               cmp_v_pos[j], cmp_v_w1[j], cmp_v_w2[j], cos, sin)
        else:
            x = x + mamba2_mixer(h, ssm_w_in[j], ssm_conv_w[j], ssm_conv_b[j], ssm_dt_bias[j],
                                 ssm_a_log[j], ssm_d[j], ssm_norm[j], ssm_w_out[j])
        x = x + 0.5 * swiglu(rmsnorm(x, g[2]), ffn_w_gate[i, 1], ffn_w_up[i, 1], ffn_w_down[i, 1])
    return rmsnorm(x, final_norm)
```

```python
import functools
import math

import numpy as np
import jax
import jax.numpy as jnp
from jax import lax
from jax.experimental import pallas as pl
from jax.experimental.pallas import tpu as pltpu

HEAD_DIM = 64
ROPE_THETA = 10000.0
Q_BLOCK = 128
SWA_Q_HEADS = 8
SWA_KV_HEADS = 2
SWA_WINDOW = 128
NSA_Q_HEADS = 8
NSA_KV_HEADS = 2
CMP_LEN = 32
CMP_STRIDE = 16
CMP_HIDDEN = 128
SEL_LEN = 64
SEL_TOPK = 16
NSA_WINDOW = 512
N_BRANCH = 3
SSM_HEAD_DIM = 64
SSM_GROUPS = 8
D_STATE = 128
CONV_W = 4
SSD_CHUNK = 128
RMS_EPS = 1e-6
NEG_INF = -1e30
SEL_FORCE = 1e4

LANES = 128
VMEM_LIMIT = 56 * 1024 * 1024
BF16 = jnp.bfloat16
F32 = jnp.float32


def _params(*sem):
    return pltpu.CompilerParams(dimension_semantics=sem, vmem_limit_bytes=VMEM_LIMIT)


def _dot(a, b):
    return jnp.dot(a, b, preferred_element_type=F32)


def _dot_nt(a, b):
    return lax.dot_general(a, b, (((1,), (1,)), ((), ())), preferred_element_type=F32)


def _dot_tn(a, b):
    return lax.dot_general(a, b, (((0,), (0,)), ((), ())), preferred_element_type=F32)


def _split2(v):
    hi = v.astype(BF16)
    lo = (v - hi.astype(F32)).astype(BF16)
    return hi, lo


def _tile_rows(mask, rep):
    return jnp.concatenate([mask.astype(F32)] * rep, axis=0) > 0.5


def _rms(x, g):
    return x * lax.rsqrt(jnp.mean(x * x, axis=-1, keepdims=True) + RMS_EPS) * g


def _ffn_kernel(x_ref, g_ref, wg_ref, wu_ref, wd_ref, fg_ref, o_ref, h_ref, acc_ref, *, final_norm):
    f = pl.program_id(1)

    @pl.when(f == 0)
    def _():
        h_ref[...] = _rms(x_ref[...], g_ref[...]).astype(BF16)
        acc_ref[...] = jnp.zeros_like(acc_ref)

    h = h_ref[...]
    gate = _dot(h, wg_ref[...])
    up = _dot(h, wu_ref[...])
    a = (gate * jax.nn.sigmoid(gate) * up).astype(BF16)
    acc_ref[...] += _dot(a, wd_ref[...])

    @pl.when(f == pl.num_programs(1) - 1)
    def _():
        y = x_ref[...] + 0.5 * acc_ref[...]
        if final_norm:
            y = _rms(y, fg_ref[...])
        o_ref[...] = y


def _ffn(x, g, wg, wu, wd, final_gain=None, tm=1024, tf=256):
    t, d = x.shape
    f = wg.shape[1]
    tm = min(tm, t)
    fg = jnp.ones((1, d), F32) if final_gain is None else final_gain.reshape(1, d)
    return pl.pallas_call(
        functools.partial(_ffn_kernel, final_norm=final_gain is not None),
        out_shape=jax.ShapeDtypeStruct((t, d), F32),
        grid=(t // tm, f // tf),
        in_specs=[
            pl.BlockSpec((tm, d), lambda i, j: (i, 0)),
            pl.BlockSpec((1, d), lambda i, j: (0, 0)),
            pl.BlockSpec((d, tf), lambda i, j: (0, j)),
            pl.BlockSpec((d, tf), lambda i, j: (0, j)),
            pl.BlockSpec((tf, d), lambda i, j: (j, 0)),
            pl.BlockSpec((1, d), lambda i, j: (0, 0)),
        ],
        out_specs=pl.BlockSpec((tm, d), lambda i, j: (i, 0)),
        scratch_shapes=[pltpu.VMEM((tm, d), BF16), pltpu.VMEM((tm, d), F32)],
        compiler_params=_params("parallel", "arbitrary"),
        name="ffn",
    )(x, g.reshape(1, d), wg, wu, wd, fg)


def _proj_res_kernel(*refs, n_in):
    x_ref = refs[0]
    o_ref = refs[1 + 2 * n_in]
    y = x_ref[...]
    for i in range(n_in):
        y = y + _dot(refs[1 + i][...], refs[1 + n_in + i][...])
    o_ref[...] = y


def _proj_res(x, acts, ws, tm=512):
    t, d = x.shape
    tm = min(tm, t)
    n_in = len(acts)
    in_specs = [pl.BlockSpec((tm, d), lambda i: (i, 0))]
    in_specs += [pl.BlockSpec((tm, a.shape[1]), lambda i: (i, 0)) for a in acts]
    in_specs += [pl.BlockSpec(w.shape, lambda i: (0, 0)) for w in ws]
    return pl.pallas_call(
        functools.partial(_proj_res_kernel, n_in=n_in),
        out_shape=jax.ShapeDtypeStruct((t, d), F32),
        grid=(t // tm,),
        in_specs=in_specs,
        out_specs=pl.BlockSpec((tm, d), lambda i: (i, 0)),
        compiler_params=_params("parallel"),
        name="proj_res",
    )(x, *acts, *ws)


def _rope_lanes(y, c, s_lo, s_hi):
    return y * c + pltpu.roll(y, 96, 1) * s_lo + pltpu.roll(y, 32, 1) * s_hi


def _attn_in_kernel(x_ref, g_ref, c_ref, slo_ref, shi_ref,
                    wqa_ref, wkva_ref, wqb_ref, wcmp_ref, wkvs_ref, wgt_ref,
                    qa_ref, kva_ref, qb_ref, cmp_ref, kvs_ref, gt_ref):
    h = _rms(x_ref[...], g_ref[...]).astype(BF16)
    c, s_lo, s_hi = c_ref[...], slo_ref[...], shi_ref[...]
    scale = HEAD_DIM ** -0.5

    def slabs(y, roped, mult=1.0):
        out = []
        for k in range(y.shape[1] // LANES):
            blk = y[:, k * LANES:(k + 1) * LANES]
            if roped[k]:
                blk = _rope_lanes(blk, c, s_lo, s_hi)
            out.append(blk * mult if mult != 1.0 else blk)
        return jnp.concatenate(out, axis=1) if len(out) > 1 else out[0]

    qa_ref[...] = slabs(_dot(h, wqa_ref[...]), [True] * 4, scale).astype(BF16)
    kva_ref[...] = slabs(_dot(h, wkva_ref[...]), [True, False]).astype(BF16)
    qb_ref[...] = slabs(_dot(h, wqb_ref[...]), [True] * 4, scale).astype(BF16)
    cmp_ref[...] = slabs(_dot(h, wcmp_ref[...]), [True, False])
    kvs_ref[...] = slabs(_dot(h, wkvs_ref[...]), [True, False, True, False]).astype(BF16)
    gt_ref[...] = jax.nn.sigmoid(_dot(h, wgt_ref[...]))


def _rope_tables(seq):
    inv = 1.0 / (ROPE_THETA ** (jnp.arange(0, HEAD_DIM, 2, dtype=F32) / HEAD_DIM))
    ang = jnp.arange(seq, dtype=F32)[:, None] * inv[None, :]
    cos, sin = jnp.cos(ang), jnp.sin(ang)
    zero = jnp.zeros_like(sin)
    c = jnp.tile(cos, (1, 4))
    s_lo = jnp.tile(jnp.concatenate([-sin, zero], axis=1), (1, 2))
    s_hi = jnp.tile(jnp.concatenate([zero, sin], axis=1), (1, 2))
    return c, s_lo, s_hi


def _attn_in(x, g, rope, w_in, seq, tm=512):
    t, d = x.shape
    tm = min(tm, seq)
    dq = SWA_Q_HEADS * HEAD_DIM
    dkv = SWA_KV_HEADS * HEAD_DIM
    w = w_in.astype(BF16)
    cuts = np.cumsum([dq, 2 * dkv, dq, 2 * dkv, 4 * dkv])
    wqa, wkva, wqb, wcmp, wkvs, wgt = jnp.split(w, cuts, axis=1)
    wgt = jnp.pad(wgt, ((0, 0), (0, LANES - wgt.shape[1])))
    ws = [wqa, wkva, wqb, wcmp, wkvs, wgt]
    n_pos = seq // tm
    out_dtypes = [BF16, BF16, BF16, F32, BF16, F32]
    row = lambda i: (i, 0)
    return pl.pallas_call(
        _attn_in_kernel,
        out_shape=[jax.ShapeDtypeStruct((t, wi.shape[1]), dt) for wi, dt in zip(ws, out_dtypes)],
        grid=(t // tm,),
        in_specs=[pl.BlockSpec((tm, d), row), pl.BlockSpec((1, d), lambda i: (0, 0))]
        + [pl.BlockSpec((tm, LANES), lambda i: (i % n_pos, 0))] * 3
        + [pl.BlockSpec(wi.shape, lambda i: (0, 0)) for wi in ws],
        out_specs=[pl.BlockSpec((tm, wi.shape[1]), row) for wi in ws],
        compiler_params=_params("parallel"),
        name="attn_in",
    )(x, g.reshape(1, d), *rope, *ws)


def _stack_heads(q_ref, g, rep):
    return jnp.concatenate(
        [q_ref[:, (g * rep + r) * HEAD_DIM:(g * rep + r + 1) * HEAD_DIM] for r in range(rep)], axis=0)


def _swa_kernel(sink_ref, q_ref, kv_ref, o_ref):
    n = pl.program_id(1)
    rep = SWA_Q_HEADS // SWA_KV_HEADS
    nkeys = 2 * Q_BLOCK
    start = pl.multiple_of(jnp.maximum(n - 1, 0) * Q_BLOCK, Q_BLOCK)
    qpos = n * Q_BLOCK + lax.broadcasted_iota(jnp.int32, (Q_BLOCK, nkeys), 0)
    kpos = start + lax.broadcasted_iota(jnp.int32, (Q_BLOCK, nkeys), 1)
    diff = qpos - kpos
    mask = _tile_rows((diff >= 0) & (diff < SWA_WINDOW), rep)
    outs = []
    for g in range(SWA_KV_HEADS):
        qs = _stack_heads(q_ref, g, rep)
        k = kv_ref[pl.ds(start, nkeys), g * HEAD_DIM:(g + 1) * HEAD_DIM]
        v = kv_ref[pl.ds(start, nkeys), (SWA_KV_HEADS + g) * HEAD_DIM:(SWA_KV_HEADS + g + 1) * HEAD_DIM]
        s = jnp.where(mask, _dot_nt(qs, k), NEG_INF)
        row = lax.broadcasted_iota(jnp.int32, (rep * Q_BLOCK, 1), 0) // Q_BLOCK
        sink = jnp.zeros((rep * Q_BLOCK, 1), F32)
        for r in range(rep):
            sink = jnp.where(row == r, sink_ref[g * rep + r], sink)
        m = jnp.maximum(jnp.max(s, axis=-1, keepdims=True), sink)
        e = jnp.where(mask, jnp.exp(s - m), 0.0)
        den = jnp.sum(e, axis=-1, keepdims=True) + jnp.exp(sink - m)
        p = e / jnp.maximum(den, 1e-30)
        o = _dot(p.astype(BF16), v)
        outs += [o[r * Q_BLOCK:(r + 1) * Q_BLOCK] for r in range(rep)]
    o_ref[...] = jnp.concatenate(outs, axis=1).astype(BF16)


def _swa(qa, kva, sinks, bsz, seq):
    nq = seq // Q_BLOCK
    dq = qa.shape[1]
    return pl.pallas_call(
        _swa_kernel,
        out_shape=jax.ShapeDtypeStruct((bsz * seq, dq), BF16),
        grid=(bsz, nq),
        in_specs=[
            pl.BlockSpec(memory_space=pltpu.SMEM),
            pl.BlockSpec((Q_BLOCK, dq), lambda b, n: (b * nq + n, 0)),
            pl.BlockSpec((seq, kva.shape[1]), lambda b, n: (b, 0)),
        ],
        out_specs=pl.BlockSpec((Q_BLOCK, dq), lambda b, n: (b * nq + n, 0)),
        compiler_params=_params("parallel", "arbitrary"),
        name="swa",
    )(sinks, qa, kva)


def _compress_kernel(r_ref, pos_ref, w1_ref, w2_ref, o_ref):
    half = CMP_STRIDE * HEAD_DIM
    r = r_ref[...]
    top = _dot((r + pos_ref[0:1, :]).astype(BF16), w1_ref[0:half, :])
    bot = _dot((r + pos_ref[1:2, :]).astype(BF16), w1_ref[half:2 * half, :])
    pre = top + pltpu.roll(bot, bot.shape[0] - 1, 0)
    o_ref[...] = _dot(jax.nn.gelu(pre).astype(BF16), w2_ref[...]).astype(BF16)


def _compress(cmp_in, pos, w1, w2, bsz, seq):
    ng = NSA_KV_HEADS
    nr = seq // CMP_STRIDE
    half = CMP_STRIDE * HEAD_DIM
    rows = cmp_in.reshape(bsz, nr, CMP_STRIDE, 2 * ng, HEAD_DIM).transpose(0, 3, 1, 2, 4)
    rows = rows.reshape(bsz, 2 * ng, nr, half)
    return pl.pallas_call(
        _compress_kernel,
        out_shape=jax.ShapeDtypeStruct((bsz, 2 * ng, nr, HEAD_DIM), BF16),
        grid=(bsz, 2 * ng),
        in_specs=[
            pl.BlockSpec((None, None, nr, half), lambda b, j: (b, j, 0, 0)),
            pl.BlockSpec((None, 2, half), lambda b, j: (j // ng, 0, 0)),
            pl.BlockSpec((None, 2 * half, CMP_HIDDEN), lambda b, j: (j // ng, 0, 0)),
            pl.BlockSpec((None, CMP_HIDDEN, HEAD_DIM), lambda b, j: (j // ng, 0, 0)),
        ],
        out_specs=pl.BlockSpec((None, None, nr, HEAD_DIM), lambda b, j: (b, j, 0, 0)),
        compiler_params=_params("parallel", "parallel"),
        name="compress",
    )(rows, pos, w1, w2)


def _softmax_rows(s, mask):
    s = jnp.where(mask, s, NEG_INF)
    m = jnp.max(s, axis=-1, keepdims=True)
    e = jnp.where(mask, jnp.exp(s - m), 0.0)
    return e / jnp.maximum(jnp.sum(e, axis=-1, keepdims=True), 1e-30)


def _nsa_kernel(q_ref, kv_ref, cmp_ref, gt_ref, wcs_ref, exp_ref, o_ref,
                sel_ref, m_ref, l_ref, acc_ref, *, seq, kt):
    n = pl.program_id(1)
    rep = NSA_Q_HEADS // NSA_KV_HEADS
    ng = NSA_KV_HEADS
    rows = rep * Q_BLOCK
    n_cmp = seq // CMP_STRIDE
    n_blk = seq // SEL_LEN
    t0 = n * Q_BLOCK
    dh = HEAD_DIM

    def kv_cols(kind, g):
        lo = (kind * ng + g) * dh
        return slice(lo, lo + dh)

    outs = []
    for g in range(ng):
        qs = _stack_heads(q_ref, g, rep)

        kc = cmp_ref[g]
        vc = cmp_ref[ng + g]
        tq = t0 + lax.broadcasted_iota(jnp.int32, (Q_BLOCK, n_cmp), 0)
        cend = lax.broadcasted_iota(jnp.int32, (Q_BLOCK, n_cmp), 1) * CMP_STRIDE + (CMP_LEN - 1)
        p_cmp = _softmax_rows(_dot_nt(qs, kc), _tile_rows(cend <= tq, rep))
        o_cmp = _dot(p_cmp.astype(BF16), vc)
        psum = p_cmp[0:Q_BLOCK]
        for r in range(1, rep):
            psum = psum + p_cmp[r * Q_BLOCK:(r + 1) * Q_BLOCK]
        p_hi, p_lo = _split2(psum)
        wcs_t = wcs_ref[...]
        imp_t = _dot_nt(wcs_t, p_hi) + _dot_nt(wcs_t, p_lo)
        jj = lax.broadcasted_iota(jnp.int32, (n_blk, Q_BLOCK), 0)
        cur = (t0 + lax.broadcasted_iota(jnp.int32, (n_blk, Q_BLOCK), 1)) // SEL_LEN
        valid = jj <= cur
        forced = valid & ((jj == 0) | (jj == cur) | (jj == cur - 1))
        score = imp_t + jnp.where(forced, SEL_FORCE, 0.0) - jnp.where(valid, 0.0, SEL_FORCE)
        rank = jnp.zeros((n_blk, Q_BLOCK), jnp.int32)
        for j in range(n_blk):
            rj = score[j:j + 1, :]
            beats = (rj > score) | ((rj == score) & (jj > j))
            rank = rank + jnp.where(beats, 1, 0)
        sel_t = (rank < min(SEL_TOPK, n_blk)).astype(F32)
        sel_ref[...] = jnp.transpose(
            jnp.concatenate([sel_t, jnp.zeros((LANES - n_blk, Q_BLOCK), F32)], axis=0)
            if n_blk < LANES else sel_t).astype(BF16)

        m_ref[...] = jnp.full_like(m_ref, NEG_INF)
        l_ref[...] = jnp.zeros_like(l_ref)
        acc_ref[...] = jnp.zeros_like(acc_ref)

        def body(i, carry):
            k0 = pl.multiple_of(i * kt, kt)
            k = kv_ref[pl.ds(k0, kt), kv_cols(0, g)]
            v = kv_ref[pl.ds(k0, kt), kv_cols(1, g)]
            selm = _dot(sel_ref[...], exp_ref[i])
            kpos = k0 + lax.broadcasted_iota(jnp.int32, (Q_BLOCK, kt), 1)
            qpos = t0 + lax.broadcasted_iota(jnp.int32, (Q_BLOCK, kt), 0)
            mask = _tile_rows((selm > 0.5) & (kpos <= qpos), rep)
            s = jnp.where(mask, _dot_nt(qs, k), NEG_INF)
            m_old = m_ref[...]
            m_new = jnp.maximum(m_old, jnp.max(s, axis=-1, keepdims=True))
            e = jnp.where(mask, jnp.exp(s - m_new), 0.0)
            alpha = jnp.exp(m_old - m_new)
            l_ref[...] = alpha * l_ref[...] + jnp.sum(e, axis=-1, keepdims=True)
            acc_ref[...] = alpha * acc_ref[...] + _dot(e.astype(BF16), v)
            m_ref[...] = m_new
            return carry

        lax.fori_loop(0, (t0 + Q_BLOCK + kt - 1) // kt, body, 0)
        o_slc = acc_ref[...] / jnp.maximum(l_ref[...], 1e-30)

        nwin = NSA_WINDOW + Q_BLOCK
        start = pl.multiple_of(jnp.maximum(t0 + Q_BLOCK - nwin, 0), Q_BLOCK)
        kw = kv_ref[pl.ds(start, nwin), kv_cols(2, g)]
        vw = kv_ref[pl.ds(start, nwin), kv_cols(3, g)]
        diff = (t0 + lax.broadcasted_iota(jnp.int32, (Q_BLOCK, nwin), 0)
                - start - lax.broadcasted_iota(jnp.int32, (Q_BLOCK, nwin), 1))
        p_win = _softmax_rows(_dot_nt(qs, kw), _tile_rows((diff >= 0) & (diff < NSA_WINDOW), rep))
        o_win = _dot(p_win.astype(BF16), vw)

        for r in range(rep):
            sl = slice(r * Q_BLOCK, (r + 1) * Q_BLOCK)
            c0 = (g * rep + r) * N_BRANCH
            outs.append(gt_ref[:, c0:c0 + 1] * o_cmp[sl] + gt_ref[:, c0 + 1:c0 + 2] * o_slc[sl]
                        + gt_ref[:, c0 + 2:c0 + 3] * o_win[sl])
    o_ref[...] = jnp.concatenate(outs, axis=1).astype(BF16)


def _cmp_to_sel_weights_t(seq):
    n_cmp = (seq - CMP_LEN) // CMP_STRIDE + 1
    n_blk = seq // SEL_LEN
    cs = np.arange(n_cmp) * CMP_STRIDE
    ss = np.arange(n_blk) * SEL_LEN
    ov = np.minimum(cs[:, None] + CMP_LEN, ss[None, :] + SEL_LEN) - np.maximum(cs[:, None], ss[None, :])
    w = (np.clip(ov, 0, None) / CMP_LEN).astype(np.float32)
    w = np.concatenate([w, np.zeros((seq // CMP_STRIDE - n_cmp, n_blk), np.float32)], axis=0)
    return jnp.asarray(w.T, dtype=BF16)


def _nsa(qb, kvs, cmp_kv, gates, bsz, seq, kt=512):
    nq = seq // Q_BLOCK
    dq = qb.shape[1]
    kt = min(kt, seq)
    n_blk = seq // SEL_LEN
    assert n_blk <= LANES and NSA_WINDOW + Q_BLOCK <= seq
    rows = (NSA_Q_HEADS // NSA_KV_HEADS) * Q_BLOCK
    kblk = (np.arange(seq) // SEL_LEN).reshape(seq // kt, 1, kt)
    expand = jnp.asarray(kblk == np.arange(LANES).reshape(1, LANES, 1), dtype=BF16)
    return pl.pallas_call(
        functools.partial(_nsa_kernel, seq=seq, kt=kt),
        out_shape=jax.ShapeDtypeStruct((bsz * seq, dq), BF16),
        grid=(bsz, nq),
        in_specs=[
            pl.BlockSpec((Q_BLOCK, dq), lambda b, n: (b * nq + n, 0)),
            pl.BlockSpec((seq, kvs.shape[1]), lambda b, n: (b, 0)),
            pl.BlockSpec((None,) + cmp_kv.shape[1:], lambda b, n: (b, 0, 0, 0)),
            pl.BlockSpec((Q_BLOCK, LANES), lambda b, n: (b * nq + n, 0)),
            pl.BlockSpec((n_blk, seq // CMP_STRIDE), lambda b, n: (0, 0)),
            pl.BlockSpec(expand.shape, lambda b, n: (0, 0, 0)),
        ],
        out_specs=pl.BlockSpec((Q_BLOCK, dq), lambda b, n: (b * nq + n, 0)),
        scratch_shapes=[
            pltpu.VMEM((Q_BLOCK, LANES), BF16),
            pltpu.VMEM((rows, 1), F32),
            pltpu.VMEM((rows, 1), F32),
            pltpu.VMEM((rows, HEAD_DIM), F32),
        ],
        compiler_params=_params("parallel", "arbitrary"),
        name="nsa",
    )(qb, kvs, cmp_kv, gates, _cmp_to_sel_weights_t(seq), expand)


def _rms_matmul_kernel(x_ref, g_ref, w_ref, b_ref, o_ref, h_ref, *, softplus):
    @pl.when(pl.program_id(1) == 0)
    def _():
        h_ref[...] = _rms(x_ref[...], g_ref[...]).astype(BF16)

    y = _dot(h_ref[...], w_ref[...])
    if softplus:
        y = jax.nn.softplus(y + b_ref[...])
    o_ref[...] = y.astype(o_ref.dtype)


def _rms_matmul(x, g, w, bias=None, tm=512, tn=1024):
    t, d = x.shape
    n = w.shape[1]
    tm, tn = min(tm, t), min(tn, n)
    b = jnp.zeros((1, n), F32) if bias is None else bias.reshape(1, n)
    return pl.pallas_call(
        functools.partial(_rms_matmul_kernel, softplus=bias is not None),
        out_shape=jax.ShapeDtypeStruct((t, n), F32),
        grid=(t // tm, n // tn),
        in_specs=[
            pl.BlockSpec((tm, d), lambda i, j: (i, 0)),
            pl.BlockSpec((1, d), lambda i, j: (0, 0)),
            pl.BlockSpec((d, tn), lambda i, j: (0, j)),
            pl.BlockSpec((1, tn), lambda i, j: (0, j)),
        ],
        out_specs=pl.BlockSpec((tm, tn), lambda i, j: (i, j)),
        scratch_shapes=[pltpu.VMEM((tm, d), BF16)],
        compiler_params=_params("parallel", "arbitrary"),
        name="rms_matmul",
    )(x, g.reshape(1, d), w, b)


HALO = 8


def _xbc_kernel(x_ref, g_ref, w_ref, cw_ref, cb_ref, o_ref, h_ref, pad_ref, *, tiles_per_seq):
    i = pl.program_id(0)
    j = pl.program_id(1)
    tm = x_ref.shape[0]

    @pl.when(j == 0)
    def _():
        h_ref[...] = _rms(x_ref[...], g_ref[...]).astype(BF16)

    @pl.when(i % tiles_per_seq == 0)
    def _():
        pad_ref[j, 0:HALO, :] = jnp.zeros((HALO, pad_ref.shape[2]), F32)

    pad_ref[j, HALO:HALO + tm, :] = _dot(h_ref[...], w_ref[...])
    y = cb_ref[...] + cw_ref[CONV_W - 1:CONV_W, :] * pad_ref[j, HALO:HALO + tm, :]
    for k in range(1, CONV_W):
        y = y + cw_ref[CONV_W - 1 - k:CONV_W - k, :] * pad_ref[j, HALO - k:HALO - k + tm, :]
    o_ref[...] = (y * jax.nn.sigmoid(y)).astype(o_ref.dtype)
    pad_ref[j, 0:HALO, :] = pad_ref[j, tm:tm + HALO, :]


def _xbc(x, g, w, conv_w, conv_b, seq, out_dtype, tm=512, tn=1024):
    t, d = x.shape
    n = w.shape[1]
    tm, tn = min(tm, seq), min(tn, n)
    return pl.pallas_call(
        functools.partial(_xbc_kernel, tiles_per_seq=seq // tm),
        out_shape=jax.ShapeDtypeStruct((t, n), out_dtype),
        grid=(t // tm, n // tn),
        in_specs=[
            pl.BlockSpec((tm, d), lambda i, j: (i, 0)),
            pl.BlockSpec((1, d), lambda i, j: (0, 0)),
            pl.BlockSpec((d, tn), lambda i, j: (0, j)),
            pl.BlockSpec((CONV_W, tn), lambda i, j: (0, j)),
            pl.BlockSpec((1, tn), lambda i, j: (0, j)),
        ],
        out_specs=pl.BlockSpec((tm, tn), lambda i, j: (i, j)),
        scratch_shapes=[pltpu.VMEM((tm, d), BF16), pltpu.VMEM((n // tn, tm + HALO, tn), F32)],
        compiler_params=_params("arbitrary", "arbitrary"),
        name="xbc_conv",
    )(x, g.reshape(1, d), w, conv_w, conv_b.reshape(1, n))


def _ssd_kernel(xs_ref, bm_ref, cm_ref, dt_ref, z_ref, a_ref, d_ref, nw_ref, o_ref, st_ref):
    L = SSD_CHUNK
    n_heads = xs_ref.shape[1] // SSM_HEAD_DIM
    rep = n_heads // SSM_GROUPS
    gw = rep * SSM_HEAD_DIM

    @pl.when(pl.program_id(1) == 0)
    def _():
        st_ref[...] = jnp.zeros_like(st_ref)

    dt = dt_ref[...]
    da = dt * a_ref[...]
    li = lax.broadcasted_iota(jnp.int32, (L, L), 0)
    si = lax.broadcasted_iota(jnp.int32, (L, L), 1)
    causal = li >= si
    tril = causal.astype(BF16)
    hi = da.astype(BF16)
    r1 = da - hi.astype(F32)
    mid = r1.astype(BF16)
    lo = (r1 - mid.astype(F32)).astype(BF16)
    a_cs = _dot(tril, hi) + _dot(tril, mid) + _dot(tril, lo)
    a_cs_t = jnp.transpose(a_cs)
    total = a_cs[L - 1:L, :]
    ea = jnp.exp(a_cs)
    wdec = jnp.exp(total - a_cs) * dt
    cdec = jnp.exp(total)

    lane = lax.broadcasted_iota(jnp.int32, (L, LANES), 1)
    left = lane < SSM_HEAD_DIM

    def pair_cols(v, k):
        rows = v.shape[0]
        lft = left if rows == L else left[0:rows]
        return jnp.where(lft, v[:, 2 * k:2 * k + 1], v[:, 2 * k + 1:2 * k + 2])

    for g in range(SSM_GROUPS):
        bg = bm_ref[:, g * D_STATE:(g + 1) * D_STATE]
        cg = cm_ref[:, g * D_STATE:(g + 1) * D_STATE]
        cb = _dot_nt(cg, bg)
        st = st_ref[g]
        y_off = _dot(cg, st.astype(BF16))
        ys, xws, eas, cds = [], [], [], []
        for kk in range(rep // 2):
            k = g * (rep // 2) + kk
            x_pair = xs_ref[:, k * LANES:(k + 1) * LANES]
            xdt = x_pair * pair_cols(dt, k)
            y_pair = jnp.zeros((L, LANES), F32)
            for side in range(2):
                h = 2 * k + side
                seg = a_cs[:, h:h + 1] - a_cs_t[h:h + 1, :]
                m = (cb * jnp.exp(jnp.where(causal, seg, -jnp.inf))).astype(BF16)
                xh = jnp.where(left if side == 0 else ~left, xdt, 0.0).astype(BF16)
                y_pair = y_pair + _dot(m, xh)
            ys.append(y_pair)
            xws.append(x_pair * pair_cols(wdec, k))
            eas.append(pair_cols(ea, k))
            cds.append(pair_cols(cdec, k))
        cat = lambda parts: jnp.concatenate(parts, axis=1) if len(parts) > 1 else parts[0]
        y_g = cat(ys) + y_off * cat(eas)
        st_ref[g] = cat(cds) * st + _dot_tn(bg, cat(xws).astype(BF16))
        xg = xs_ref[:, g * gw:(g + 1) * gw]
        zg = z_ref[:, g * gw:(g + 1) * gw]
        gy = (y_g + d_ref[:, g * gw:(g + 1) * gw] * xg) * (zg * jax.nn.sigmoid(zg))
        gy = gy * lax.rsqrt(jnp.mean(gy * gy, axis=-1, keepdims=True) + RMS_EPS)
        o_ref[:, g * gw:(g + 1) * gw] = (gy * nw_ref[:, g * gw:(g + 1) * gw]).astype(BF16)


def _ssd(xs, bm, cm, dt, z, a, d_exp, norm_w, bsz, seq):
    t, di = xs.shape
    nc = seq // SSD_CHUNK
    gw = di // SSM_GROUPS
    assert di % (SSM_GROUPS * LANES) == 0 and gw == di // SSM_GROUPS
    blk = lambda width: pl.BlockSpec((SSD_CHUNK, width), lambda b, c: (b * nc + c, 0))
    const = lambda width: pl.BlockSpec((1, width), lambda b, c: (0, 0))
    return pl.pallas_call(
        _ssd_kernel,
        out_shape=jax.ShapeDtypeStruct((t, di), BF16),
        grid=(bsz, nc),
        in_specs=[blk(di), blk(bm.shape[1]), blk(cm.shape[1]), blk(LANES), blk(di),
                  const(LANES), const(di), const(di)],
        out_specs=blk(di),
        scratch_shapes=[pltpu.VMEM((SSM_GROUPS, D_STATE, gw), F32)],
        compiler_params=_params("parallel", "arbitrary"),
        name="ssd",
    )(xs, bm, cm, dt, z, a, d_exp, norm_w)


def _attn_layer(x, g, rope, w_in, w_out, sinks, ckp, ckw1, ckw2, cvp, cvw1, cvw2, bsz, seq):
    qa, kva, qb, cmp_in, kvs, gates = _attn_in(x, g, rope, w_in, seq)
    o_a = _swa(qa, kva, sinks, bsz, seq)
    half = CMP_STRIDE * HEAD_DIM
    pos = jnp.stack([ckp, cvp]).reshape(2, 2, half)
    w1 = jnp.stack([ckw1, cvw1]).astype(BF16)
    w2 = jnp.stack([ckw2, cvw2]).astype(BF16)
    cmp_kv = _compress(cmp_in, pos, w1, w2, bsz, seq)
    o_b = _nsa(qb, kvs, cmp_kv, gates, bsz, seq)
    w = w_out.astype(BF16)
    da = o_a.shape[1]
    return _proj_res(x, [o_a, o_b], [w[:da], w[da:]])


def _mamba_layer(x, g, w_in, conv_w, conv_b, dt_bias, a_log, d_skip, norm_w, w_out, bsz, seq):
    n_heads = dt_bias.shape[0]
    di = n_heads * SSM_HEAD_DIM
    gn = SSM_GROUPS * D_STATE
    w = w_in.astype(BF16)
    wz, wx, wb, wc, wdt = jnp.split(w, np.cumsum([di, di, gn, gn]), axis=1)
    cwx, cwb, cwc = jnp.split(conv_w, np.cumsum([di, gn]), axis=1)
    cbx, cbb, cbc = jnp.split(conv_b, np.cumsum([di, gn]))
    z = _rms_matmul(x, g, wz)
    xs = _xbc(x, g, wx, cwx, cbx, seq, F32)
    bm = _xbc(x, g, wb, cwb, cbb, seq, BF16)
    cm = _xbc(x, g, wc, cwc, cbc, seq, BF16)
    padh = LANES - n_heads
    dt = _rms_matmul(x, g, jnp.pad(wdt, ((0, 0), (0, padh))), bias=jnp.pad(dt_bias, (0, padh)))
    a = jnp.pad(-jnp.exp(a_log), (0, padh)).reshape(1, LANES)
    d_exp = jnp.repeat(d_skip, SSM_HEAD_DIM).reshape(1, di)
    yn = _ssd(xs, bm, cm, dt, z, a, d_exp, norm_w.reshape(1, di), bsz, seq)
    return _proj_res(x, [yn], [w_out.astype(BF16)])


def kernel(x, norm_gains, final_norm, ffn_w_gate, ffn_w_up, ffn_w_down, attn_w_in, attn_w_out, attn_sinks, cmp_k_pos, cmp_k_w1, cmp_k_w2, cmp_v_pos, cmp_v_w1, cmp_v_w2, ssm_w_in, ssm_conv_w, ssm_conv_b, ssm_dt_bias, ssm_a_log, ssm_d, ssm_norm, ssm_w_out):
    bsz, seq, d = x.shape
    depth = norm_gains.shape[0]
    rope = _rope_tables(seq)
    wg, wu, wd = (w.astype(BF16) for w in (ffn_w_gate, ffn_w_up, ffn_w_down))
    x = x.reshape(bsz * seq, d)
    for i in range(depth):
        g = norm_gains[i]
        x = _ffn(x, g[0], wg[i, 0], wu[i, 0], wd[i, 0])
        j = i // 2
        if i % 2 == 0:
            x = _attn_layer(x, g[1], rope, attn_w_in[j], attn_w_out[j], attn_sinks[j],
                            cmp_k_pos[j], cmp_k_w1[j], cmp_k_w2[j],
                            cmp_v_pos[j], cmp_v_w1[j], cmp_v_w2[j], bsz, seq)
        else:
            x = _mamba_layer(x, g[1], ssm_w_in[j], ssm_conv_w[j], ssm_conv_b[j], ssm_dt_bias[j],
                             ssm_a_log[j], ssm_d[j], ssm_norm[j], ssm_w_out[j], bsz, seq)
        x = _ffn(x, g[2], wg[i, 1], wu[i, 1], wd[i, 1],
                 final_gain=final_norm if i == depth - 1 else None)
    return x.reshape(bsz, seq, d)
```

```python
import functools
import math

import numpy as np
import jax
import jax.numpy as jnp
from jax import lax
from jax.experimental import pallas as pl
from jax.experimental.pallas import tpu as pltpu

HEAD_DIM = 64
ROPE_THETA = 10000.0
Q_BLOCK = 128
SWA_Q_HEADS = 8
SWA_KV_HEADS = 2
SWA_WINDOW = 128
NSA_Q_HEADS = 8
NSA_KV_HEADS = 2
CMP_LEN = 32
CMP_STRIDE = 16
CMP_HIDDEN = 128
SEL_LEN = 64
SEL_TOPK = 16
NSA_WINDOW = 512
N_BRANCH = 3
SSM_HEAD_DIM = 64
SSM_GROUPS = 8
D_STATE = 128
CONV_W = 4
SSD_CHUNK = 128
RMS_EPS = 1e-6
NEG_INF = -1e30
SEL_FORCE = 1e4

LANES = 128
VMEM_LIMIT = 56 * 1024 * 1024
BF16 = jnp.bfloat16
F32 = jnp.float32


def _params(*sem):
    return pltpu.CompilerParams(dimension_semantics=sem, vmem_limit_bytes=VMEM_LIMIT)


def _dot(a, b):
    return jnp.dot(a, b, preferred_element_type=F32)


def _dot_nt(a, b):
    return lax.dot_general(a, b, (((1,), (1,)), ((), ())), preferred_element_type=F32)


def _dot_tn(a, b):
    return lax.dot_general(a, b, (((0,), (0,)), ((), ())), preferred_element_type=F32)


def _split2(v):
    hi = v.astype(BF16)
    lo = (v - hi.astype(F32)).astype(BF16)
    return hi, lo


def _rms(x, g):
    return x * lax.rsqrt(jnp.mean(x * x, axis=-1, keepdims=True) + RMS_EPS) * g


def _ffn_kernel(x_ref, g_ref, wg_ref, wu_ref, wd_ref, fg_ref, o_ref, h_ref, acc_ref, *, final_norm):
    f = pl.program_id(1)

    @pl.when(f == 0)
    def _():
        h_ref[...] = _rms(x_ref[...], g_ref[...]).astype(BF16)
        acc_ref[...] = jnp.zeros_like(acc_ref)

    h = h_ref[...]
    gate = _dot(h, wg_ref[...])
    up = _dot(h, wu_ref[...])
    a = (gate * jax.nn.sigmoid(gate) * up).astype(BF16)
    acc_ref[...] += _dot(a, wd_ref[...])

    @pl.when(f == pl.num_programs(1) - 1)
    def _():
        y = x_ref[...] + 0.5 * acc_ref[...]
        if final_norm:
            y = _rms(y, fg_ref[...])
        o_ref[...] = y


def _ffn(x, g, wg, wu, wd, final_gain=None, tm=1024, tf=256):
    t, d = x.shape
    f = wg.shape[1]
    tm = min(tm, t)
    fg = jnp.ones((1, d), F32) if final_gain is None else final_gain.reshape(1, d)
    return pl.pallas_call(
        functools.partial(_ffn_kernel, final_norm=final_gain is not None),
        out_shape=jax.ShapeDtypeStruct((t, d), F32),
        grid=(t // tm, f // tf),
        in_specs=[
            pl.BlockSpec((tm, d), lambda i, j: (i, 0)),
            pl.BlockSpec((1, d), lambda i, j: (0, 0)),
            pl.BlockSpec((d, tf), lambda i, j: (0, j)),
            pl.BlockSpec((d, tf), lambda i, j: (0, j)),
            pl.BlockSpec((tf, d), lambda i, j: (j, 0)),
            pl.BlockSpec((1, d), lambda i, j: (0, 0)),
        ],
        out_specs=pl.BlockSpec((tm, d), lambda i, j: (i, 0)),
        scratch_shapes=[pltpu.VMEM((tm, d), BF16), pltpu.VMEM((tm, d), F32)],
        compiler_params=_params("parallel", "arbitrary"),
        name="ffn",
    )(x, g.reshape(1, d), wg, wu, wd, fg)


def _proj_res_kernel(*refs, n_in):
    x_ref = refs[0]
    o_ref = refs[1 + 2 * n_in]
    y = x_ref[...]
    for i in range(n_in):
        y = y + _dot(refs[1 + i][...], refs[1 + n_in + i][...])
    o_ref[...] = y


def _proj_res(x, acts, ws, tm=512):
    t, d = x.shape
    tm = min(tm, t)
    n_in = len(acts)
    in_specs = [pl.BlockSpec((tm, d), lambda i: (i, 0))]
    in_specs += [pl.BlockSpec((tm, a.shape[1]), lambda i: (i, 0)) for a in acts]
    in_specs += [pl.BlockSpec(w.shape, lambda i: (0, 0)) for w in ws]
    return pl.pallas_call(
        functools.partial(_proj_res_kernel, n_in=n_in),
        out_shape=jax.ShapeDtypeStruct((t, d), F32),
        grid=(t // tm,),
        in_specs=in_specs,
        out_specs=pl.BlockSpec((tm, d), lambda i: (i, 0)),
        compiler_params=_params("parallel"),
        name="proj_res",
    )(x, *acts, *ws)


def _rope_lanes(y, c, s_lo, s_hi):
    return y * c + pltpu.roll(y, 96, 1) * s_lo + pltpu.roll(y, 32, 1) * s_hi


GATE_ROWS = 32
HALF = HEAD_DIM // 2


def _attn_in_kernel(x_ref, g_ref, c_ref, slo_ref, shi_ref, ct_ref, st_ref,
                    wk_ref, wcmp_ref, wqt_ref, wvt_ref, wgt_ref,
                    k_ref, cmp_ref, qt_ref, vt_ref, gt_ref):
    h = _rms(x_ref[...], g_ref[...]).astype(BF16)
    c, s_lo, s_hi = c_ref[...], slo_ref[...], shi_ref[...]
    yk = _dot(h, wk_ref[...])
    k_ref[...] = jnp.concatenate(
        [_rope_lanes(yk[:, i * LANES:(i + 1) * LANES], c, s_lo, s_hi) for i in range(3)], axis=1).astype(BF16)
    yc = _dot(h, wcmp_ref[...])
    cmp_ref[...] = jnp.concatenate([_rope_lanes(yc[:, :LANES], c, s_lo, s_hi), yc[:, LANES:]], axis=1)

    ct, st = ct_ref[...], st_ref[...]
    scale = HEAD_DIM ** -0.5
    yq = _dot_nt(wqt_ref[...], h)
    for hd in range(yq.shape[0] // HEAD_DIM):
        lo = hd * HEAD_DIM
        y1, y2 = yq[lo:lo + HALF], yq[lo + HALF:lo + HEAD_DIM]
        qt_ref[lo:lo + HALF, :] = ((y1 * ct - y2 * st) * scale).astype(BF16)
        qt_ref[lo + HALF:lo + HEAD_DIM, :] = ((y2 * ct + y1 * st) * scale).astype(BF16)
    yv = _dot_nt(wvt_ref[...], h).astype(BF16)
    for cidx in range(yv.shape[1] // Q_BLOCK):
        vt_ref[cidx] = yv[:, cidx * Q_BLOCK:(cidx + 1) * Q_BLOCK]
    gt_ref[...] = jax.nn.sigmoid(_dot_nt(wgt_ref[...], h))


def _rope_tables(seq):
    inv = 1.0 / (ROPE_THETA ** (jnp.arange(0, HEAD_DIM, 2, dtype=F32) / HEAD_DIM))
    ang = jnp.arange(seq, dtype=F32)[:, None] * inv[None, :]
    cos, sin = jnp.cos(ang), jnp.sin(ang)
    zero = jnp.zeros_like(sin)
    c = jnp.tile(cos, (1, 4))
    s_lo = jnp.tile(jnp.concatenate([-sin, zero], axis=1), (1, 2))
    s_hi = jnp.tile(jnp.concatenate([zero, sin], axis=1), (1, 2))
    return c, s_lo, s_hi, cos.T, sin.T


def _attn_in(x, g, rope, w_in, seq, tm=512):
    t, d = x.shape
    tm = min(tm, seq)
    dq = SWA_Q_HEADS * HEAD_DIM
    dkv = SWA_KV_HEADS * HEAD_DIM
    w = w_in.astype(BF16)
    cuts = np.cumsum([dq, dkv, dkv, dq] + [dkv] * 6)
    wqa, wka, wva, wqb, wkc, wvc, wks, wvs, wkw, wvw, wgt = jnp.split(w, cuts, axis=1)
    wk = jnp.concatenate([wka, wks, wkw], axis=1)
    wcmp = jnp.concatenate([wkc, wvc], axis=1)
    wqt = jnp.concatenate([wqa, wqb], axis=1).T
    wvt = jnp.concatenate([wva, wvs, wvw], axis=1).T
    wgt = jnp.pad(wgt, ((0, 0), (0, GATE_ROWS - wgt.shape[1]))).T
    ws = [wk, wcmp, wqt, wvt, wgt]
    n_pos = seq // tm
    nvt = tm // Q_BLOCK
    row = lambda i: (i, 0)
    col = lambda i: (0, i)
    return pl.pallas_call(
        _attn_in_kernel,
        out_shape=[
            jax.ShapeDtypeStruct((t, wk.shape[1]), BF16),
            jax.ShapeDtypeStruct((t, wcmp.shape[1]), F32),
            jax.ShapeDtypeStruct((wqt.shape[0], t), BF16),
            jax.ShapeDtypeStruct((t // Q_BLOCK, wvt.shape[0], Q_BLOCK), BF16),
            jax.ShapeDtypeStruct((GATE_ROWS, t), F32),
        ],
        grid=(t // tm,),
        in_specs=[pl.BlockSpec((tm, d), row), pl.BlockSpec((1, d), lambda i: (0, 0))]
        + [pl.BlockSpec((tm, LANES), lambda i: (i % n_pos, 0))] * 3
        + [pl.BlockSpec((HALF, tm), lambda i: (0, i % n_pos))] * 2
        + [pl.BlockSpec(wi.shape, lambda i: (0, 0)) for wi in ws],
        out_specs=[
            pl.BlockSpec((tm, wk.shape[1]), row),
            pl.BlockSpec((tm, wcmp.shape[1]), row),
            pl.BlockSpec((wqt.shape[0], tm), col),
            pl.BlockSpec((nvt, wvt.shape[0], Q_BLOCK), lambda i: (i, 0, 0)),
            pl.BlockSpec((GATE_ROWS, tm), col),
        ],
        compiler_params=_params("parallel"),
        name="attn_in",
    )(x, g.reshape(1, d), *rope, *ws)


def _group_q(qt_ref, g, rep):
    qs = jnp.concatenate(
        [qt_ref[(g * rep + r) * HEAD_DIM:(g * rep + r + 1) * HEAD_DIM, :] for r in range(rep)], axis=1)
    z = jnp.zeros_like(qs)
    return qs, jnp.concatenate([qs, z] if g == 0 else [z, qs], axis=0)


def _store_heads(o_ref, ot, g, rep):
    for p in range(rep // 2):
        pair = jnp.concatenate([ot[:, (2 * p) * Q_BLOCK:(2 * p + 1) * Q_BLOCK],
                                ot[:, (2 * p + 1) * Q_BLOCK:(2 * p + 2) * Q_BLOCK]], axis=0)
        c0 = (g * (rep // 2) + p) * LANES
        o_ref[:, c0:c0 + LANES] = jnp.transpose(pair).astype(BF16)


def _vt_tiles(vt_ref, tile0, ntiles, kind, g):
    lo = (kind * 2 + g) * HEAD_DIM
    return jnp.concatenate([vt_ref[tile0 + c, lo:lo + HEAD_DIM, :] for c in range(ntiles)], axis=1)


def _swa_kernel(sink_ref, qt_ref, k_ref, vt_ref, o_ref):
    n = pl.program_id(1)
    rep = SWA_Q_HEADS // SWA_KV_HEADS
    nq = rep * Q_BLOCK
    nkeys = 2 * Q_BLOCK
    tile0 = jnp.maximum(n - 1, 0)
    start = pl.multiple_of(tile0 * Q_BLOCK, Q_BLOCK)
    lane = lax.broadcasted_iota(jnp.int32, (nkeys, nq), 1)
    diff = n * Q_BLOCK + (lane & (Q_BLOCK - 1)) - start - lax.broadcasted_iota(jnp.int32, (nkeys, nq), 0)
    mask = (diff >= 0) & (diff < SWA_WINDOW)
    head = lax.broadcasted_iota(jnp.int32, (1, nq), 1) // Q_BLOCK
    k2 = k_ref[pl.ds(start, nkeys), 0:LANES]
    for g in range(SWA_KV_HEADS):
        _, qs2 = _group_q(qt_ref, g, rep)
        s = jnp.where(mask, _dot(k2, qs2), NEG_INF)
        sink = jnp.zeros((1, nq), F32)
        for r in range(rep):
            sink = jnp.where(head == r, sink_ref[g * rep + r], sink)
        m = jnp.maximum(jnp.max(s, axis=0, keepdims=True), sink)
        e = jnp.where(mask, jnp.exp(s - m), 0.0)
        den = jnp.sum(e, axis=0, keepdims=True) + jnp.exp(sink - m)
        ot = _dot(_vt_tiles(vt_ref, tile0, 2, 0, g), e.astype(BF16)) / jnp.maximum(den, 1e-30)
        _store_heads(o_ref, ot, g, rep)


def _swa(qt, k, vt, sinks, bsz, seq):
    nq = seq // Q_BLOCK
    dq = SWA_Q_HEADS * HEAD_DIM
    return pl.pallas_call(
        _swa_kernel,
        out_shape=jax.ShapeDtypeStruct((bsz * seq, dq), BF16),
        grid=(bsz, nq),
        in_specs=[
            pl.BlockSpec(memory_space=pltpu.SMEM),
            pl.BlockSpec((dq, Q_BLOCK), lambda b, n: (0, b * nq + n)),
            pl.BlockSpec((seq, k.shape[1]), lambda b, n: (b, 0)),
            pl.BlockSpec((nq,) + vt.shape[1:], lambda b, n: (b, 0, 0)),
        ],
        out_specs=pl.BlockSpec((Q_BLOCK, dq), lambda b, n: (b * nq + n, 0)),
        compiler_params=_params("parallel", "arbitrary"),
        name="swa",
    )(sinks, qt, k, vt)


def _compress_kernel(r_ref, pos_ref, w1_ref, w2_ref, w2t_ref, o_ref, ot_ref):
    half = CMP_STRIDE * HEAD_DIM
    r = r_ref[...]
    top = _dot((r + pos_ref[0:1, :]).astype(BF16), w1_ref[0:half, :])
    bot = _dot((r + pos_ref[1:2, :]).astype(BF16), w1_ref[half:2 * half, :])
    pre = top + pltpu.roll(bot, bot.shape[0] - 1, 0)
    hid = jax.nn.gelu(pre).astype(BF16)
    o_ref[...] = _dot(hid, w2_ref[...]).astype(BF16)
    ot_ref[...] = _dot_nt(w2t_ref[...], hid).astype(BF16)


def _compress(cmp_in, pos, w1, w2, bsz, seq):
    ng = NSA_KV_HEADS
    nr = seq // CMP_STRIDE
    half = CMP_STRIDE * HEAD_DIM
    rows = cmp_in.reshape(bsz, nr, CMP_STRIDE, 2 * ng, HEAD_DIM).transpose(0, 3, 1, 2, 4)
    rows = rows.reshape(bsz, 2 * ng, nr, half)
    return pl.pallas_call(
        _compress_kernel,
        out_shape=[jax.ShapeDtypeStruct((bsz, 2 * ng, nr, HEAD_DIM), BF16),
                   jax.ShapeDtypeStruct((bsz, 2 * ng, HEAD_DIM, nr), BF16)],
        grid=(bsz, 2 * ng),
        in_specs=[
            pl.BlockSpec((None, None, nr, half), lambda b, j: (b, j, 0, 0)),
            pl.BlockSpec((None, 2, half), lambda b, j: (j // ng, 0, 0)),
            pl.BlockSpec((None, 2 * half, CMP_HIDDEN), lambda b, j: (j // ng, 0, 0)),
            pl.BlockSpec((None, CMP_HIDDEN, HEAD_DIM), lambda b, j: (j // ng, 0, 0)),
            pl.BlockSpec((None, HEAD_DIM, CMP_HIDDEN), lambda b, j: (j // ng, 0, 0)),
        ],
        out_specs=[pl.BlockSpec((None, None, nr, HEAD_DIM), lambda b, j: (b, j, 0, 0)),
                   pl.BlockSpec((None, None, HEAD_DIM, nr), lambda b, j: (b, j, 0, 0))],
        compiler_params=_params("parallel", "parallel"),
        name="compress",
    )(rows, pos, w1, w2, jnp.swapaxes(w2, 1, 2))


def _softmax_cols(s, mask):
    s = jnp.where(mask, s, NEG_INF)
    m = jnp.max(s, axis=0, keepdims=True)
    e = jnp.where(mask, jnp.exp(s - m), 0.0)
    return e, jnp.maximum(jnp.sum(e, axis=0, keepdims=True), 1e-30)


def _stable_topk_mask(score, k):
    nb = score.shape[0]
    jj = lax.broadcasted_iota(jnp.int32, (8, score.shape[1]), 0)
    rank = jnp.zeros(score.shape, F32)
    for j in range(nb):
        rj = score[j:j + 1, :]
        v0 = (j // 8) * 8
        mid = score[v0:v0 + 8]
        parts = [jnp.where((rj > mid) | ((rj == mid) & (jj + v0 > j)), 1.0, 0.0)]
        if v0 > 0:
            parts.insert(0, jnp.where(rj > score[0:v0], 1.0, 0.0))
        if v0 + 8 < nb:
            parts.append(jnp.where(rj >= score[v0 + 8:nb], 1.0, 0.0))
        rank = rank + (jnp.concatenate(parts, axis=0) if len(parts) > 1 else parts[0])
    return jnp.where(rank < k, 1.0, 0.0)


def _nsa_kernel(qt_ref, k_ref, vt_ref, cn_ref, ct_ref, gt_ref, wcs_ref, o_ref, sel_ref, *, seq, kt):
    n = pl.program_id(1)
    rep = NSA_Q_HEADS // NSA_KV_HEADS
    ng = NSA_KV_HEADS
    nq = rep * Q_BLOCK
    n_cmp = seq // CMP_STRIDE
    n_blk = seq // SEL_LEN
    bpt = kt // SEL_LEN
    tpk = kt // Q_BLOCK
    t0 = n * Q_BLOCK
    tq = t0 + (lax.broadcasted_iota(jnp.int32, (1, nq), 1) & (Q_BLOCK - 1))

    for g in range(ng):
        qs, qs2 = _group_q(qt_ref, g, rep)

        cend = lax.broadcasted_iota(jnp.int32, (n_cmp, nq), 0) * CMP_STRIDE + (CMP_LEN - 1)
        e_cmp, den_cmp = _softmax_cols(_dot(cn_ref[g], qs), cend <= tq)
        p_cmp = e_cmp / den_cmp
        o_cmp = _dot(ct_ref[ng + g], p_cmp.astype(BF16))
        psum = p_cmp[:, 0:Q_BLOCK]
        for r in range(1, rep):
            psum = psum + p_cmp[:, r * Q_BLOCK:(r + 1) * Q_BLOCK]
        p_hi, p_lo = _split2(psum)
        imp = _dot(wcs_ref[...], p_hi) + _dot(wcs_ref[...], p_lo)
        jb = lax.broadcasted_iota(jnp.int32, (n_blk, Q_BLOCK), 0)
        cur = (t0 + lax.broadcasted_iota(jnp.int32, (n_blk, Q_BLOCK), 1)) // SEL_LEN
        valid = jb <= cur
        forced = valid & ((jb == 0) | (jb == cur) | (jb == cur - 1))
        score = imp + jnp.where(forced, SEL_FORCE, 0.0) - jnp.where(valid, 0.0, SEL_FORCE)
        sel = _stable_topk_mask(score, min(SEL_TOPK, n_blk))
        sel_ref[...] = jnp.concatenate([sel] * rep, axis=1)

        def tile(i, carry, diagonal):
            m, l, acc = carry
            k0 = pl.multiple_of(i * kt, kt)
            s = _dot(k_ref[pl.ds(k0, kt), LANES:2 * LANES], qs2)
            selb = sel_ref[pl.ds(pl.multiple_of(i * bpt, bpt), bpt), :]
            s = jnp.concatenate(
                [jnp.where(selb[b:b + 1, :] > 0.5, s[b * SEL_LEN:(b + 1) * SEL_LEN, :], NEG_INF)
                 for b in range(bpt)], axis=0)
            if diagonal:
                kpos = k0 + lax.broadcasted_iota(jnp.int32, (kt, nq), 0)
                s = jnp.where(kpos <= tq, s, NEG_INF)
            m_new = jnp.maximum(m, jnp.max(s, axis=0, keepdims=True))
            e = jnp.exp(s - jnp.where(m_new > NEG_INF, m_new, 0.0))
            alpha = jnp.exp(m - m_new)
            l = alpha * l + jnp.sum(e, axis=0, keepdims=True)
            acc = alpha * acc + _dot(_vt_tiles(vt_ref, i * tpk, tpk, 1, g), e.astype(BF16))
            return m_new, l, acc

        init = (jnp.full((1, nq), NEG_INF, F32), jnp.zeros((1, nq), F32), jnp.zeros((HEAD_DIM, nq), F32))
        n_full = t0 // kt
        carry = lax.fori_loop(0, n_full, lambda i, c: tile(i, c, False), init)
        _, l, acc = tile(n_full, carry, True)
        o_slc = acc / jnp.maximum(l, 1e-30)

        nwin = NSA_WINDOW + Q_BLOCK
        wt0 = jnp.maximum(n + 1 - nwin // Q_BLOCK, 0)
        start = pl.multiple_of(wt0 * Q_BLOCK, Q_BLOCK)
        diff = tq - start - lax.broadcasted_iota(jnp.int32, (nwin, nq), 0)
        e_win, den_win = _softmax_cols(_dot(k_ref[pl.ds(start, nwin), 2 * LANES:3 * LANES], qs2),
                                       (diff >= 0) & (diff < NSA_WINDOW))
        o_win = _dot(_vt_tiles(vt_ref, wt0, nwin // Q_BLOCK, 2, g), e_win.astype(BF16)) / den_win

        gate = lambda c: jnp.concatenate(
            [gt_ref[(g * rep + r) * N_BRANCH + c:(g * rep + r) * N_BRANCH + c + 1, :] for r in range(rep)], axis=1)
        _store_heads(o_ref, gate(0) * o_cmp + gate(1) * o_slc + gate(2) * o_win, g, rep)


def _cmp_to_sel_weights_t(seq):
    n_cmp = (seq - CMP_LEN) // CMP_STRIDE + 1
    n_blk = seq // SEL_LEN
    cs = np.arange(n_cmp) * CMP_STRIDE
    ss = np.arange(n_blk) * SEL_LEN
    ov = np.minimum(cs[:, None] + CMP_LEN, ss[None, :] + SEL_LEN) - np.maximum(cs[:, None], ss[None, :])
    w = (np.clip(ov, 0, None) / CMP_LEN).astype(np.float32)
    w = np.concatenate([w, np.zeros((seq // CMP_STRIDE - n_cmp, n_blk), np.float32)], axis=0)
    return jnp.asarray(w.T, dtype=BF16)


def _nsa(qt, k, vt, cmp_n, cmp_t, gates_t, bsz, seq, kt=512):
    nq = seq // Q_BLOCK
    dq = NSA_Q_HEADS * HEAD_DIM
    kt = min(kt, seq)
    n_blk = seq // SEL_LEN
    assert NSA_WINDOW + Q_BLOCK <= seq and (kt // SEL_LEN) % 8 == 0
    return pl.pallas_call(
        functools.partial(_nsa_kernel, seq=seq, kt=kt),
        out_shape=jax.ShapeDtypeStruct((bsz * seq, dq), BF16),
        grid=(bsz, nq),
        in_specs=[
            pl.BlockSpec((dq, Q_BLOCK), lambda b, n: (1, b * nq + n)),
            pl.BlockSpec((seq, k.shape[1]), lambda b, n: (b, 0)),
            pl.BlockSpec((nq,) + vt.shape[1:], lambda b, n: (b, 0, 0)),
            pl.BlockSpec((None,) + cmp_n.shape[1:], lambda b, n: (b, 0, 0, 0)),
            pl.BlockSpec((None,) + cmp_t.shape[1:], lambda b, n: (b, 0, 0, 0)),
            pl.BlockSpec((GATE_ROWS, Q_BLOCK), lambda b, n: (0, b * nq + n)),
            pl.BlockSpec((n_blk, seq // CMP_STRIDE), lambda b, n: (0, 0)),
        ],
        out_specs=pl.BlockSpec((Q_BLOCK, dq), lambda b, n: (b * nq + n, 0)),
        scratch_shapes=[pltpu.VMEM((n_blk, (NSA_Q_HEADS // NSA_KV_HEADS) * Q_BLOCK), F32)],
        compiler_params=_params("parallel", "arbitrary"),
        name="nsa",
    )(qt, k, vt, cmp_n, cmp_t, gates_t, _cmp_to_sel_weights_t(seq))


def _rms_matmul_kernel(x_ref, g_ref, w_ref, b_ref, o_ref, h_ref, *, softplus):
    @pl.when(pl.program_id(1) == 0)
    def _():
        h_ref[...] = _rms(x_ref[...], g_ref[...]).astype(BF16)

    y = _dot(h_ref[...], w_ref[...])
    if softplus:
        y = jax.nn.softplus(y + b_ref[...])
    o_ref[...] = y.astype(o_ref.dtype)


def _rms_matmul(x, g, w, bias=None, tm=512, tn=1024):
    t, d = x.shape
    n = w.shape[1]
    tm, tn = min(tm, t), min(tn, n)
    b = jnp.zeros((1, n), F32) if bias is None else bias.reshape(1, n)
    return pl.pallas_call(
        functools.partial(_rms_matmul_kernel, softplus=bias is not None),
        out_shape=jax.ShapeDtypeStruct((t, n), F32),
        grid=(t // tm, n // tn),
        in_specs=[
            pl.BlockSpec((tm, d), lambda i, j: (i, 0)),
            pl.BlockSpec((1, d), lambda i, j: (0, 0)),
            pl.BlockSpec((d, tn), lambda i, j: (0, j)),
            pl.BlockSpec((1, tn), lambda i, j: (0, j)),
        ],
        out_specs=pl.BlockSpec((tm, tn), lambda i, j: (i, j)),
        scratch_shapes=[pltpu.VMEM((tm, d), BF16)],
        compiler_params=_params("parallel", "arbitrary"),
        name="rms_matmul",
    )(x, g.reshape(1, d), w, b)


HALO = 8


def _xbc_kernel(x_ref, g_ref, w_ref, cw_ref, cb_ref, o_ref, h_ref, pad_ref, *, tiles_per_seq):
    i = pl.program_id(0)
    j = pl.program_id(1)
    tm = x_ref.shape[0]

    @pl.when(j == 0)
    def _():
        h_ref[...] = _rms(x_ref[...], g_ref[...]).astype(BF16)

    @pl.when(i % tiles_per_seq == 0)
    def _():
        pad_ref[j, 0:HALO, :] = jnp.zeros((HALO, pad_ref.shape[2]), F32)

    pad_ref[j, HALO:HALO + tm, :] = _dot(h_ref[...], w_ref[...])
    y = cb_ref[...] + cw_ref[CONV_W - 1:CONV_W, :] * pad_ref[j, HALO:HALO + tm, :]
    for k in range(1, CONV_W):
        y = y + cw_ref[CONV_W - 1 - k:CONV_W - k, :] * pad_ref[j, HALO - k:HALO - k + tm, :]
    o_ref[...] = (y * jax.nn.sigmoid(y)).astype(o_ref.dtype)
    pad_ref[j, 0:HALO, :] = pad_ref[j, tm:tm + HALO, :]


def _xbc(x, g, w, conv_w, conv_b, seq, out_dtype, tm=512, tn=1024):
    t, d = x.shape
    n = w.shape[1]
    tm, tn = min(tm, seq), min(tn, n)
    return pl.pallas_call(
        functools.partial(_xbc_kernel, tiles_per_seq=seq // tm),
        out_shape=jax.ShapeDtypeStruct((t, n), out_dtype),
        grid=(t // tm, n // tn),
        in_specs=[
            pl.BlockSpec((tm, d), lambda i, j: (i, 0)),
            pl.BlockSpec((1, d), lambda i, j: (0, 0)),
            pl.BlockSpec((d, tn), lambda i, j: (0, j)),
            pl.BlockSpec((CONV_W, tn), lambda i, j: (0, j)),
            pl.BlockSpec((1, tn), lambda i, j: (0, j)),
        ],
        out_specs=pl.BlockSpec((tm, tn), lambda i, j: (i, j)),
        scratch_shapes=[pltpu.VMEM((tm, d), BF16), pltpu.VMEM((n // tn, tm + HALO, tn), F32)],
        compiler_params=_params("arbitrary", "arbitrary"),
        name="xbc_conv",
    )(x, g.reshape(1, d), w, conv_w, conv_b.reshape(1, n))


def _ssd_kernel(xs_ref, bm_ref, cm_ref, dt_ref, z_ref, a_ref, d_ref, nw_ref, o_ref, st_ref):
    L = SSD_CHUNK
    n_heads = xs_ref.shape[1] // SSM_HEAD_DIM
    rep = n_heads // SSM_GROUPS
    gw = rep * SSM_HEAD_DIM

    @pl.when(pl.program_id(1) == 0)
    def _():
        st_ref[...] = jnp.zeros_like(st_ref)

    dt = dt_ref[...]
    da = dt * a_ref[...]
    li = lax.broadcasted_iota(jnp.int32, (L, L), 0)
    si = lax.broadcasted_iota(jnp.int32, (L, L), 1)
    causal = li >= si
    tril = causal.astype(BF16)
    hi = da.astype(BF16)
    r1 = da - hi.astype(F32)
    mid = r1.astype(BF16)
    lo = (r1 - mid.astype(F32)).astype(BF16)
    a_cs = _dot(tril, hi) + _dot(tril, mid) + _dot(tril, lo)
    a_cs_t = jnp.transpose(a_cs)
    total = a_cs[L - 1:L, :]
    ea = jnp.exp(a_cs)
    wdec = jnp.exp(total - a_cs) * dt
    cdec = jnp.exp(total)

    lane = lax.broadcasted_iota(jnp.int32, (L, LANES), 1)
    left = lane < SSM_HEAD_DIM

    def pair_cols(v, k):
        rows = v.shape[0]
        lft = left if rows == L else left[0:rows]
        return jnp.where(lft, v[:, 2 * k:2 * k + 1], v[:, 2 * k + 1:2 * k + 2])

    for g in range(SSM_GROUPS):
        bg = bm_ref[:, g * D_STATE:(g + 1) * D_STATE]
        cg = cm_ref[:, g * D_STATE:(g + 1) * D_STATE]
        cb = _dot_nt(cg, bg)
        st = st_ref[g]
        y_off = _dot(cg, st.astype(BF16))
        ys, xws, eas, cds = [], [], [], []
        for kk in range(rep // 2):
            k = g * (rep // 2) + kk
            x_pair = xs_ref[:, k * LANES:(k + 1) * LANES]
            xdt = x_pair * pair_cols(dt, k)
            y_pair = jnp.zeros((L, LANES), F32)
            for side in range(2):
                h = 2 * k + side
                seg = a_cs[:, h:h + 1] - a_cs_t[h:h + 1, :]
                m = (cb * jnp.exp(jnp.where(causal, seg, -jnp.inf))).astype(BF16)
                xh = jnp.where(left if side == 0 else ~left, xdt, 0.0).astype(BF16)
                y_pair = y_pair + _dot(m, xh)
            ys.append(y_pair)
            xws.append(x_pair * pair_cols(wdec, k))
            eas.append(pair_cols(ea, k))
            cds.append(pair_cols(cdec, k))
        cat = lambda parts: jnp.concatenate(parts, axis=1) if len(parts) > 1 else parts[0]
        y_g = cat(ys) + y_off * cat(eas)
        st_ref[g] = cat(cds) * st + _dot_tn(bg, cat(xws).astype(BF16))
        xg = xs_ref[:, g * gw:(g + 1) * gw]
        zg = z_ref[:, g * gw:(g + 1) * gw]
        gy = (y_g + d_ref[:, g * gw:(g + 1) * gw] * xg) * (zg * jax.nn.sigmoid(zg))
        gy = gy * lax.rsqrt(jnp.mean(gy * gy, axis=-1, keepdims=True) + RMS_EPS)
        o_ref[:, g * gw:(g + 1) * gw] = (gy * nw_ref[:, g * gw:(g + 1) * gw]).astype(BF16)


def _ssd(xs, bm, cm, dt, z, a, d_exp, norm_w, bsz, seq):
    t, di = xs.shape
    nc = seq // SSD_CHUNK
    gw = di // SSM_GROUPS
    assert di % (SSM_GROUPS * LANES) == 0 and gw == di // SSM_GROUPS
    blk = lambda width: pl.BlockSpec((SSD_CHUNK, width), lambda b, c: (b * nc + c, 0))
    const = lambda width: pl.BlockSpec((1, width), lambda b, c: (0, 0))
    return pl.pallas_call(
        _ssd_kernel,
        out_shape=jax.ShapeDtypeStruct((t, di), BF16),
        grid=(bsz, nc),
        in_specs=[blk(di), blk(bm.shape[1]), blk(cm.shape[1]), blk(LANES), blk(di),
                  const(LANES), const(di), const(di)],
        out_specs=blk(di),
        scratch_shapes=[pltpu.VMEM((SSM_GROUPS, D_STATE, gw), F32)],
        compiler_params=_params("parallel", "arbitrary"),
        name="ssd",
    )(xs, bm, cm, dt, z, a, d_exp, norm_w)


def _attn_layer(x, g, rope, w_in, w_out, sinks, ckp, ckw1, ckw2, cvp, cvw1, cvw2, bsz, seq):
    k, cmp_in, qt, vt, gates_t = _attn_in(x, g, rope, w_in, seq)
    o_a = _swa(qt, k, vt, sinks, bsz, seq)
    half = CMP_STRIDE * HEAD_DIM
    pos = jnp.stack([ckp, cvp]).reshape(2, 2, half)
    w1 = jnp.stack([ckw1, cvw1]).astype(BF16)
    w2 = jnp.stack([ckw2, cvw2]).astype(BF16)
    cmp_n, cmp_t = _compress(cmp_in, pos, w1, w2, bsz, seq)
    o_b = _nsa(qt, k, vt, cmp_n, cmp_t, gates_t, bsz, seq)
    w = w_out.astype(BF16)
    da = o_a.shape[1]
    return _proj_res(x, [o_a, o_b], [w[:da], w[da:]])


def _mamba_layer(x, g, w_in, conv_w, conv_b, dt_bias, a_log, d_skip, norm_w, w_out, bsz, seq):
    n_heads = dt_bias.shape[0]
    di = n_heads * SSM_HEAD_DIM
    gn = SSM_GROUPS * D_STATE
    w = w_in.astype(BF16)
    wz, wx, wb, wc, wdt = jnp.split(w, np.cumsum([di, di, gn, gn]), axis=1)
    cwx, cwb, cwc = jnp.split(conv_w, np.cumsum([di, gn]), axis=1)
    cbx, cbb, cbc = jnp.split(conv_b, np.cumsum([di, gn]))
    z = _rms_matmul(x, g, wz)
    xs = _xbc(x, g, wx, cwx, cbx, seq, F32)
    bm = _xbc(x, g, wb, cwb, cbb, seq, BF16)
    cm = _xbc(x, g, wc, cwc, cbc, seq, BF16)
    padh = LANES - n_heads
    dt = _rms_matmul(x, g, jnp.pad(wdt, ((0, 0), (0, padh))), bias=jnp.pad(dt_bias, (0, padh)))
    a = jnp.pad(-jnp.exp(a_log), (0, padh)).reshape(1, LANES)
    d_exp = jnp.repeat(d_skip, SSM_HEAD_DIM).reshape(1, di)
    yn = _ssd(xs, bm, cm, dt, z, a, d_exp, norm_w.reshape(1, di), bsz, seq)
    return _proj_res(x, [yn], [w_out.astype(BF16)])


def kernel(x, norm_gains, final_norm, ffn_w_gate, ffn_w_up, ffn_w_down, attn_w_in, attn_w_out, attn_sinks, cmp_k_pos, cmp_k_w1, cmp_k_w2, cmp_v_pos, cmp_v_w1, cmp_v_w2, ssm_w_in, ssm_conv_w, ssm_conv_b, ssm_dt_bias, ssm_a_log, ssm_d, ssm_norm, ssm_w_out):
    bsz, seq, d = x.shape
    depth = norm_gains.shape[0]
    rope = _rope_tables(seq)
    wg, wu, wd = (w.astype(BF16) for w in (ffn_w_gate, ffn_w_up, ffn_w_down))
    x = x.reshape(bsz * seq, d)
    for i in range(depth):
        g = norm_gains[i]
        x = _ffn(x, g[0], wg[i, 0], wu[i, 0], wd[i, 0])
        j = i // 2
        if i % 2 == 0:
            x = _attn_layer(x, g[1], rope, attn_w_in[j], attn_w_out[j], attn_sinks[j],
                            cmp_k_pos[j], cmp_k_w1[j], cmp_k_w2[j],
                            cmp_v_pos[j], cmp_v_w1[j], cmp_v_w2[j], bsz, seq)
        else:
            x = _mamba_layer(x, g[1], ssm_w_in[j], ssm_conv_w[j], ssm_conv_b[j], ssm_dt_bias[j],
                             ssm_a_log[j], ssm_d[j], ssm_norm[j], ssm_w_out[j], bsz, seq)
        x = _ffn(x, g[2], wg[i, 1], wu[i, 1], wd[i, 1],
                 final_gain=final_norm if i == depth - 1 else None)
    return x.reshape(bsz, seq, d)
```

```python
import functools
import math

import numpy as np
import jax
import jax.numpy as jnp
from jax import lax
from jax.experimental import pallas as pl
from jax.experimental.pallas import tpu as pltpu

HEAD_DIM = 64
ROPE_THETA = 10000.0
Q_BLOCK = 128
SWA_Q_HEADS = 8
SWA_KV_HEADS = 2
SWA_WINDOW = 128
NSA_Q_HEADS = 8
NSA_KV_HEADS = 2
CMP_LEN = 32
CMP_STRIDE = 16
CMP_HIDDEN = 128
SEL_LEN = 64
SEL_TOPK = 16
NSA_WINDOW = 512
N_BRANCH = 3
SSM_HEAD_DIM = 64
SSM_GROUPS = 8
D_STATE = 128
CONV_W = 4
SSD_CHUNK = 128
RMS_EPS = 1e-6
NEG_INF = -1e30
SEL_FORCE = 1e4
LOG2E = math.log2(math.e)

LANES = 128
VMEM_LIMIT = 56 * 1024 * 1024
BF16 = jnp.bfloat16
F32 = jnp.float32


def _params(*sem, flags=None):
    return pltpu.CompilerParams(dimension_semantics=sem, vmem_limit_bytes=VMEM_LIMIT, flags=flags)


def _dot(a, b):
    return jnp.dot(a, b, preferred_element_type=F32)


def _dot_nt(a, b):
    return lax.dot_general(a, b, (((1,), (1,)), ((), ())), preferred_element_type=F32)


def _dot_tn(a, b):
    return lax.dot_general(a, b, (((0,), (0,)), ((), ())), preferred_element_type=F32)


def _split2(v):
    hi = v.astype(BF16)
    lo = (v - hi.astype(F32)).astype(BF16)
    return hi, lo


def _silu(x):
    hx = 0.5 * x
    return hx + hx * jnp.tanh(hx)


def _sigmoid(x):
    return 0.5 + 0.5 * jnp.tanh(0.5 * x)


def _rms(x, g):
    return x * lax.rsqrt(jnp.mean(x * x, axis=-1, keepdims=True) + RMS_EPS) * g


def _ffn_kernel(x_ref, g_ref, wg_ref, wu_ref, wd_ref, fg_ref, o_ref, *, final_norm, tf):
    x = x_ref[...]
    h = _rms(x, g_ref[...]).astype(BF16)
    acc = None
    for f0 in range(0, wg_ref.shape[1], tf):
        gate = _dot(h, wg_ref[:, f0:f0 + tf])
        up = _dot(h, wu_ref[:, f0:f0 + tf])
        part = _dot((_silu(gate) * up).astype(BF16), wd_ref[f0:f0 + tf, :])
        acc = part if acc is None else acc + part
    y = x + 0.5 * acc
    if final_norm:
        y = _rms(y, fg_ref[...])
    o_ref[...] = y


def _ffn(x, g, wg, wu, wd, layer, slot, final_gain=None, tm=512, tf=256):
    t, d = x.shape
    f = wg.shape[-1]
    tm = min(tm, t)
    fg = jnp.ones((1, d), F32) if final_gain is None else final_gain.reshape(1, d)
    once = pl.Buffered(1)
    return pl.pallas_call(
        functools.partial(_ffn_kernel, final_norm=final_gain is not None, tf=tf),
        out_shape=jax.ShapeDtypeStruct((t, d), F32),
        grid=(t // tm,),
        in_specs=[
            pl.BlockSpec((tm, d), lambda i: (i, 0)),
            pl.BlockSpec((1, d), lambda i: (0, 0)),
            pl.BlockSpec((None, None, d, f), lambda i: (layer, slot, 0, 0), pipeline_mode=once),
            pl.BlockSpec((None, None, d, f), lambda i: (layer, slot, 0, 0), pipeline_mode=once),
            pl.BlockSpec((None, None, f, d), lambda i: (layer, slot, 0, 0), pipeline_mode=once),
            pl.BlockSpec((1, d), lambda i: (0, 0)),
        ],
        out_specs=pl.BlockSpec((tm, d), lambda i: (i, 0)),
        compiler_params=_params("parallel"),
        name="ffn",
    )(x, g.reshape(1, d), wg, wu, wd, fg)


def _proj_res_kernel(*refs, n_in):
    x_ref = refs[0]
    o_ref = refs[1 + 2 * n_in]
    y = x_ref[...]
    for i in range(n_in):
        y = y + _dot(refs[1 + i][...], refs[1 + n_in + i][...])
    o_ref[...] = y


def _proj_res(x, acts, ws, tm=512):
    t, d = x.shape
    tm = min(tm, t)
    n_in = len(acts)
    in_specs = [pl.BlockSpec((tm, d), lambda i: (i, 0))]
    in_specs += [pl.BlockSpec((tm, a.shape[1]), lambda i: (i, 0)) for a in acts]
    in_specs += [pl.BlockSpec(w.shape, lambda i: (0, 0)) for w in ws]
    return pl.pallas_call(
        functools.partial(_proj_res_kernel, n_in=n_in),
        out_shape=jax.ShapeDtypeStruct((t, d), F32),
        grid=(t // tm,),
        in_specs=in_specs,
        out_specs=pl.BlockSpec((tm, d), lambda i: (i, 0)),
        compiler_params=_params("parallel"),
        name="proj_res",
    )(x, *acts, *ws)


def _rope_lanes(y, c, s_lo, s_hi):
    return y * c + pltpu.roll(y, 96, 1) * s_lo + pltpu.roll(y, 32, 1) * s_hi


GATE_ROWS = 32
HALF = HEAD_DIM // 2


def _attn_in_kernel(x_ref, g_ref, c_ref, slo_ref, shi_ref, ct_ref, st_ref,
                    wk_ref, wcmp_ref, wqt_ref, wvt_ref, wgt_ref,
                    k_ref, cmp_ref, qt_ref, vt_ref, gt_ref):
    h = _rms(x_ref[...], g_ref[...]).astype(BF16)
    c, s_lo, s_hi = c_ref[...], slo_ref[...], shi_ref[...]
    yk = _dot(h, wk_ref[...])
    k_ref[...] = jnp.concatenate(
        [_rope_lanes(yk[:, i * LANES:(i + 1) * LANES], c, s_lo, s_hi) for i in range(3)], axis=1).astype(BF16)
    yc = _dot(h, wcmp_ref[...])
    cmp_ref[...] = jnp.concatenate([_rope_lanes(yc[:, :LANES], c, s_lo, s_hi), yc[:, LANES:]], axis=1)

    ct, st = ct_ref[...], st_ref[...]
    scale = HEAD_DIM ** -0.5 * LOG2E
    yq = _dot_nt(wqt_ref[...], h)
    for hd in range(yq.shape[0] // HEAD_DIM):
        lo = hd * HEAD_DIM
        y1, y2 = yq[lo:lo + HALF], yq[lo + HALF:lo + HEAD_DIM]
        qt_ref[lo:lo + HALF, :] = ((y1 * ct - y2 * st) * scale).astype(BF16)
        qt_ref[lo + HALF:lo + HEAD_DIM, :] = ((y2 * ct + y1 * st) * scale).astype(BF16)
    yv = _dot_nt(wvt_ref[...], h).astype(BF16)
    for cidx in range(yv.shape[1] // Q_BLOCK):
        vt_ref[cidx] = yv[:, cidx * Q_BLOCK:(cidx + 1) * Q_BLOCK]
    gt_ref[...] = _sigmoid(_dot_nt(wgt_ref[...], h))


def _rope_tables(seq):
    inv = 1.0 / (ROPE_THETA ** (jnp.arange(0, HEAD_DIM, 2, dtype=F32) / HEAD_DIM))
    ang = jnp.arange(seq, dtype=F32)[:, None] * inv[None, :]
    cos, sin = jnp.cos(ang), jnp.sin(ang)
    zero = jnp.zeros_like(sin)
    c = jnp.tile(cos, (1, 4))
    s_lo = jnp.tile(jnp.concatenate([-sin, zero], axis=1), (1, 2))
    s_hi = jnp.tile(jnp.concatenate([zero, sin], axis=1), (1, 2))
    return c, s_lo, s_hi, cos.T, sin.T


def _attn_in(x, g, rope, w_in, seq, tm=512):
    t, d = x.shape
    tm = min(tm, seq)
    dq = SWA_Q_HEADS * HEAD_DIM
    dkv = SWA_KV_HEADS * HEAD_DIM
    w = w_in.astype(BF16)
    cuts = np.cumsum([dq, dkv, dkv, dq] + [dkv] * 6)
    wqa, wka, wva, wqb, wkc, wvc, wks, wvs, wkw, wvw, wgt = jnp.split(w, cuts, axis=1)
    wk = jnp.concatenate([wka, wks, wkw], axis=1)
    wcmp = jnp.concatenate([wkc, wvc], axis=1)
    wqt = jnp.concatenate([wqa, wqb], axis=1).T
    wvt = jnp.concatenate([wva, wvs, wvw], axis=1).T
    wgt = jnp.pad(wgt, ((0, 0), (0, GATE_ROWS - wgt.shape[1]))).T
    ws = [wk, wcmp, wqt, wvt, wgt]
    n_pos = seq // tm
    nvt = tm // Q_BLOCK
    row = lambda i: (i, 0)
    col = lambda i: (0, i)
    return pl.pallas_call(
        _attn_in_kernel,
        out_shape=[
            jax.ShapeDtypeStruct((t, wk.shape[1]), BF16),
            jax.ShapeDtypeStruct((t, wcmp.shape[1]), F32),
            jax.ShapeDtypeStruct((wqt.shape[0], t), BF16),
            jax.ShapeDtypeStruct((t // Q_BLOCK, wvt.shape[0], Q_BLOCK), BF16),
            jax.ShapeDtypeStruct((GATE_ROWS, t), F32),
        ],
        grid=(t // tm,),
        in_specs=[pl.BlockSpec((tm, d), row), pl.BlockSpec((1, d), lambda i: (0, 0))]
        + [pl.BlockSpec((tm, LANES), lambda i: (i % n_pos, 0))] * 3
        + [pl.BlockSpec((HALF, tm), lambda i: (0, i % n_pos))] * 2
        + [pl.BlockSpec(wi.shape, lambda i: (0, 0)) for wi in ws],
        out_specs=[
            pl.BlockSpec((tm, wk.shape[1]), row),
            pl.BlockSpec((tm, wcmp.shape[1]), row),
            pl.BlockSpec((wqt.shape[0], tm), col),
            pl.BlockSpec((nvt, wvt.shape[0], Q_BLOCK), lambda i: (i, 0, 0)),
            pl.BlockSpec((GATE_ROWS, tm), col),
        ],
        compiler_params=_params("parallel"),
        name="attn_in",
    )(x, g.reshape(1, d), *rope, *ws)


def _group_q(qt_ref, g, rep):
    qs = jnp.concatenate(
        [qt_ref[(g * rep + r) * HEAD_DIM:(g * rep + r + 1) * HEAD_DIM, :] for r in range(rep)], axis=1)
    z = jnp.zeros_like(qs)
    return qs, jnp.concatenate([qs, z] if g == 0 else [z, qs], axis=0)


def _store_heads(o_ref, ot, g, rep):
    for p in range(rep // 2):
        pair = jnp.concatenate([ot[:, (2 * p) * Q_BLOCK:(2 * p + 1) * Q_BLOCK],
                                ot[:, (2 * p + 1) * Q_BLOCK:(2 * p + 2) * Q_BLOCK]], axis=0)
        c0 = (g * (rep // 2) + p) * LANES
        o_ref[:, c0:c0 + LANES] = jnp.transpose(pair).astype(BF16)


SUM_ROWS = 16


def _vt_tiles(vt_ref, tile0, ntiles, kind, g):
    lo = (kind * 2 + g) * HEAD_DIM
    vt = jnp.concatenate([vt_ref[tile0 + c, lo:lo + HEAD_DIM, :] for c in range(ntiles)], axis=1)
    return jnp.concatenate([vt, jnp.ones((SUM_ROWS, vt.shape[1]), BF16)], axis=0)


def _swa_kernel(sink_ref, qt_ref, k_ref, vt_ref, o_ref):
    n = pl.program_id(1)
    rep = SWA_Q_HEADS // SWA_KV_HEADS
    nq = rep * Q_BLOCK
    nkeys = 2 * Q_BLOCK
    tile0 = jnp.maximum(n - 1, 0)
    start = pl.multiple_of(tile0 * Q_BLOCK, Q_BLOCK)
    lane = lax.broadcasted_iota(jnp.int32, (nkeys, nq), 1)
    diff = n * Q_BLOCK + (lane & (Q_BLOCK - 1)) - start - lax.broadcasted_iota(jnp.int32, (nkeys, nq), 0)
    mask = (diff >= 0) & (diff < SWA_WINDOW)
    head = lax.broadcasted_iota(jnp.int32, (1, nq), 1) // Q_BLOCK
    k2 = k_ref[pl.ds(start, nkeys), 0:LANES]
    for g in range(SWA_KV_HEADS):
        _, qs2 = _group_q(qt_ref, g, rep)
        s = jnp.where(mask, _dot(k2, qs2), NEG_INF)
        sink = jnp.zeros((1, nq), F32)
        for r in range(rep):
            sink = jnp.where(head == r, sink_ref[g * rep + r] * LOG2E, sink)
        m = jnp.maximum(jnp.max(s, axis=0, keepdims=True), sink)
        e = jnp.exp2(s - jnp.where(m > NEG_INF, m, 0.0))
        ot = _dot(_vt_tiles(vt_ref, tile0, 2, 0, g), e.astype(BF16))
        den = ot[HEAD_DIM:HEAD_DIM + 1] + jnp.exp2(sink - m)
        _store_heads(o_ref, ot[0:HEAD_DIM] / jnp.maximum(den, 1e-30), g, rep)


def _swa(qt, k, vt, sinks, bsz, seq):
    nq = seq // Q_BLOCK
    dq = SWA_Q_HEADS * HEAD_DIM
    return pl.pallas_call(
        _swa_kernel,
        out_shape=jax.ShapeDtypeStruct((bsz * seq, dq), BF16),
        grid=(bsz, nq),
        in_specs=[
            pl.BlockSpec(memory_space=pltpu.SMEM),
            pl.BlockSpec((dq, Q_BLOCK), lambda b, n: (0, b * nq + n)),
            pl.BlockSpec((seq, k.shape[1]), lambda b, n: (b, 0)),
            pl.BlockSpec((nq,) + vt.shape[1:], lambda b, n: (b, 0, 0)),
        ],
        out_specs=pl.BlockSpec((Q_BLOCK, dq), lambda b, n: (b * nq + n, 0)),
        compiler_params=_params("parallel", "arbitrary"),
        name="swa",
    )(sinks, qt, k, vt)


def _compress_kernel(r_ref, pos_ref, w1_ref, w2_ref, w2t_ref, o_ref, ot_ref):
    half = CMP_STRIDE * HEAD_DIM
    r = r_ref[...]
    top = _dot((r + pos_ref[0:1, :]).astype(BF16), w1_ref[0:half, :])
    bot = _dot((r + pos_ref[1:2, :]).astype(BF16), w1_ref[half:2 * half, :])
    pre = top + pltpu.roll(bot, bot.shape[0] - 1, 0)
    hid = jax.nn.gelu(pre).astype(BF16)
    o_ref[...] = _dot(hid, w2_ref[...]).astype(BF16)
    ot_ref[...] = _dot_nt(w2t_ref[...], hid).astype(BF16)


def _compress(cmp_in, pos, w1, w2, bsz, seq):
    ng = NSA_KV_HEADS
    nr = seq // CMP_STRIDE
    half = CMP_STRIDE * HEAD_DIM
    rows = cmp_in.reshape(bsz, nr, CMP_STRIDE, 2 * ng, HEAD_DIM).transpose(0, 3, 1, 2, 4)
    rows = rows.reshape(bsz, 2 * ng, nr, half)
    return pl.pallas_call(
        _compress_kernel,
        out_shape=[jax.ShapeDtypeStruct((bsz, 2 * ng, nr, HEAD_DIM), BF16),
                   jax.ShapeDtypeStruct((bsz, 2 * ng, HEAD_DIM, nr), BF16)],
        grid=(bsz, 2 * ng),
        in_specs=[
            pl.BlockSpec((None, None, nr, half), lambda b, j: (b, j, 0, 0)),
            pl.BlockSpec((None, 2, half), lambda b, j: (j // ng, 0, 0)),
            pl.BlockSpec((None, 2 * half, CMP_HIDDEN), lambda b, j: (j // ng, 0, 0)),
            pl.BlockSpec((None, CMP_HIDDEN, HEAD_DIM), lambda b, j: (j // ng, 0, 0)),
            pl.BlockSpec((None, HEAD_DIM, CMP_HIDDEN), lambda b, j: (j // ng, 0, 0)),
        ],
        out_specs=[pl.BlockSpec((None, None, nr, HEAD_DIM), lambda b, j: (b, j, 0, 0)),
                   pl.BlockSpec((None, None, HEAD_DIM, nr), lambda b, j: (b, j, 0, 0))],
        compiler_params=_params("parallel", "parallel"),
        name="compress",
    )(rows, pos, w1, w2, jnp.swapaxes(w2, 1, 2))


def _softmax_weights(s, mask):
    s = jnp.where(mask, s, NEG_INF)
    m = jnp.max(s, axis=0, keepdims=True)
    return jnp.exp2(s - jnp.where(m > NEG_INF, m, 0.0))


def _stable_topk_mask(score, k):
    nb = score.shape[0]
    jj = lax.broadcasted_iota(jnp.int32, (8, score.shape[1]), 0)
    rank = jnp.zeros(score.shape, F32)
    for j in range(nb):
        rj = score[j:j + 1, :]
        v0 = (j // 8) * 8
        mid = score[v0:v0 + 8]
        parts = [jnp.where((rj > mid) | ((rj == mid) & (jj + v0 > j)), 1.0, 0.0)]
        if v0 > 0:
            parts.insert(0, jnp.where(rj > score[0:v0], 1.0, 0.0))
        if v0 + 8 < nb:
            parts.append(jnp.where(rj >= score[v0 + 8:nb], 1.0, 0.0))
        rank = rank + (jnp.concatenate(parts, axis=0) if len(parts) > 1 else parts[0])
    return jnp.where(rank < k, 1.0, 0.0)


def _nsa_kernel(qt_ref, k_ref, vt_ref, cn_ref, ct_ref, gt_ref, wcs_ref, o_ref, sel_ref, part_ref,
                m_ref, acc_ref, *, seq, kt):
    n = pl.program_id(1)
    rep = NSA_Q_HEADS // NSA_KV_HEADS
    ng = NSA_KV_HEADS
    nq = rep * Q_BLOCK
    n_cmp = seq // CMP_STRIDE
    n_blk = seq // SEL_LEN
    bpt = kt // SEL_LEN
    tpk = kt // Q_BLOCK
    t0 = n * Q_BLOCK
    tq = t0 + (lax.broadcasted_iota(jnp.int32, (1, nq), 1) & (Q_BLOCK - 1))

    def gate(g, c):
        return jnp.concatenate(
            [gt_ref[(g * rep + r) * N_BRANCH + c:(g * rep + r) * N_BRANCH + c + 1, :] for r in range(rep)], axis=1)

    qs2s = []
    cmp_mask = lax.broadcasted_iota(jnp.int32, (n_cmp, nq), 0) * CMP_STRIDE + (CMP_LEN - 1) <= tq
    nwin = NSA_WINDOW + Q_BLOCK
    wt0 = jnp.maximum(n + 1 - nwin // Q_BLOCK, 0)
    start = pl.multiple_of(wt0 * Q_BLOCK, Q_BLOCK)
    diff = tq - start - lax.broadcasted_iota(jnp.int32, (nwin, nq), 0)
    win_mask = (diff >= 0) & (diff < NSA_WINDOW)
    for g in range(ng):
        qs, qs2 = _group_q(qt_ref, g, rep)
        qs2s.append(qs2)
        e_cmp = _softmax_weights(_dot(cn_ref[g], qs), cmp_mask)
        p_cmp = e_cmp / jnp.maximum(jnp.sum(e_cmp, axis=0, keepdims=True), 1e-30)
        o_cmp = _dot(ct_ref[ng + g], p_cmp.astype(BF16))
        psum = p_cmp[:, 0:Q_BLOCK]
        for r in range(1, rep):
            psum = psum + p_cmp[:, r * Q_BLOCK:(r + 1) * Q_BLOCK]
        p_hi, p_lo = _split2(psum)
        imp = _dot(wcs_ref[...], p_hi) + _dot(wcs_ref[...], p_lo)
        jb = lax.broadcasted_iota(jnp.int32, (n_blk, Q_BLOCK), 0)
        cur = (t0 + lax.broadcasted_iota(jnp.int32, (n_blk, Q_BLOCK), 1)) // SEL_LEN
        valid = jb <= cur
        forced = valid & ((jb == 0) | (jb == cur) | (jb == cur - 1))
        score = imp + jnp.where(forced, SEL_FORCE, 0.0) - jnp.where(valid, 0.0, SEL_FORCE)
        sel = _stable_topk_mask(score, min(SEL_TOPK, n_blk))
        sel_ref[g] = jnp.concatenate([sel] * rep, axis=1)

        e_win = _softmax_weights(_dot(k_ref[pl.ds(start, nwin), 2 * LANES:3 * LANES], qs2), win_mask)
        o_win = _dot(_vt_tiles(vt_ref, wt0, nwin // Q_BLOCK, 2, g), e_win.astype(BF16))
        o_win = o_win[0:HEAD_DIM] / jnp.maximum(o_win[HEAD_DIM:HEAD_DIM + 1], 1e-30)
        part_ref[g] = gate(g, 0) * o_cmp + gate(g, 2) * o_win

    def scores(i):
        kblk = k_ref[pl.ds(pl.multiple_of(i * kt, kt), kt), LANES:2 * LANES]
        return tuple(_dot(kblk, qs2s[g]) for g in range(ng))

    def tile(i, s_all, carry, diagonal):
        if diagonal:
            causal = i * kt + lax.broadcasted_iota(jnp.int32, (kt, nq), 0) <= tq
        for g in range(ng):
            m = m_ref[g]
            s = s_all[g]
            selb = sel_ref[g, pl.ds(pl.multiple_of(i * bpt, bpt), bpt), :]
            s = jnp.concatenate(
                [jnp.where(selb[b:b + 1, :] > 0.5, s[b * SEL_LEN:(b + 1) * SEL_LEN, :], NEG_INF)
                 for b in range(bpt)], axis=0)
            if diagonal:
                s = jnp.where(causal, s, NEG_INF)
            m_new = jnp.maximum(m, jnp.max(s, axis=0, keepdims=True))
            e = jnp.exp2(s - jnp.where(m_new > NEG_INF, m_new, 0.0))
            acc_ref[g] = (jnp.exp2(m - m_new) * acc_ref[g]
                          + _dot(_vt_tiles(vt_ref, i * tpk, tpk, 1, g), e.astype(BF16)))
            m_ref[g] = m_new
        return carry

    m_ref[...] = jnp.full(m_ref.shape, NEG_INF, F32)
    acc_ref[...] = jnp.zeros(acc_ref.shape, F32)
    n_full = t0 // kt
    lax.fori_loop(0, n_full, lambda i, c: tile(i, scores(i), c, False), 0)
    tile(n_full, scores(n_full), 0, True)
    for g in range(ng):
        acc = acc_ref[g]
        o_slc = acc[0:HEAD_DIM] / jnp.maximum(acc[HEAD_DIM:HEAD_DIM + 1], 1e-30)
        _store_heads(o_ref, part_ref[g] + gate(g, 1) * o_slc, g, rep)


def _cmp_to_sel_weights_t(seq):
    n_cmp = (seq - CMP_LEN) // CMP_STRIDE + 1
    n_blk = seq // SEL_LEN
    cs = np.arange(n_cmp) * CMP_STRIDE
    ss = np.arange(n_blk) * SEL_LEN
    ov = np.minimum(cs[:, None] + CMP_LEN, ss[None, :] + SEL_LEN) - np.maximum(cs[:, None], ss[None, :])
    w = (np.clip(ov, 0, None) / CMP_LEN).astype(np.float32)
    w = np.concatenate([w, np.zeros((seq // CMP_STRIDE - n_cmp, n_blk), np.float32)], axis=0)
    return jnp.asarray(w.T, dtype=BF16)


def _nsa(qt, k, vt, cmp_n, cmp_t, gates_t, bsz, seq, kt=512):
    nq = seq // Q_BLOCK
    dq = NSA_Q_HEADS * HEAD_DIM
    kt = min(kt, seq)
    n_blk = seq // SEL_LEN
    assert NSA_WINDOW + Q_BLOCK <= seq and (kt // SEL_LEN) % 8 == 0
    return pl.pallas_call(
        functools.partial(_nsa_kernel, seq=seq, kt=kt),
        out_shape=jax.ShapeDtypeStruct((bsz * seq, dq), BF16),
        grid=(bsz, nq),
        in_specs=[
            pl.BlockSpec((dq, Q_BLOCK), lambda b, n: (1, b * nq + n)),
            pl.BlockSpec((seq, k.shape[1]), lambda b, n: (b, 0)),
            pl.BlockSpec((nq,) + vt.shape[1:], lambda b, n: (b, 0, 0)),
            pl.BlockSpec((None,) + cmp_n.shape[1:], lambda b, n: (b, 0, 0, 0)),
            pl.BlockSpec((None,) + cmp_t.shape[1:], lambda b, n: (b, 0, 0, 0)),
            pl.BlockSpec((GATE_ROWS, Q_BLOCK), lambda b, n: (0, b * nq + n)),
            pl.BlockSpec((n_blk, seq // CMP_STRIDE), lambda b, n: (0, 0)),
        ],
        out_specs=pl.BlockSpec((Q_BLOCK, dq), lambda b, n: (b * nq + n, 0)),
        scratch_shapes=[pltpu.VMEM((NSA_KV_HEADS, n_blk, (NSA_Q_HEADS // NSA_KV_HEADS) * Q_BLOCK), F32),
                        pltpu.VMEM((NSA_KV_HEADS, HEAD_DIM, (NSA_Q_HEADS // NSA_KV_HEADS) * Q_BLOCK), F32),
                        pltpu.VMEM((NSA_KV_HEADS, 1, (NSA_Q_HEADS // NSA_KV_HEADS) * Q_BLOCK), F32),
                        pltpu.VMEM((NSA_KV_HEADS, HEAD_DIM + SUM_ROWS,
                                    (NSA_Q_HEADS // NSA_KV_HEADS) * Q_BLOCK), F32)],
        compiler_params=_params("parallel", "arbitrary"),
        name="nsa",
    )(qt, k, vt, cmp_n, cmp_t, gates_t, _cmp_to_sel_weights_t(seq))


HALO = 8
CONV_COLS = 512
CONV_ROWS = 32


def _mamba_in_kernel(x_ref, g_ref, wz_ref, wxbc_ref, wdt_ref, cw_ref, cb_ref, dtb_ref,
                     z_ref, xs_ref, bm_ref, cm_ref, dt_ref, pad_ref, h_ref, *, tiles_per_seq):
    tm = x_ref.shape[0]
    di = xs_ref.shape[1]
    gn = bm_ref.shape[1]
    n = pad_ref.shape[1]

    @pl.when(pl.program_id(0) % tiles_per_seq == 0)
    def _():
        pad_ref[0:HALO, :] = jnp.zeros((HALO, n), F32)

    h_ref[...] = _rms(x_ref[...], g_ref[...]).astype(BF16)
    dt_ref[...] = jax.nn.softplus(_dot(h_ref[...], wdt_ref[...]) + dtb_ref[...])
    zc = di * CONV_COLS // n
    cn, rb = CONV_COLS, CONV_ROWS
    for c0 in range(0, n, cn):
        pad_ref[HALO:HALO + tm, c0:c0 + cn] = _dot(h_ref[...], wxbc_ref[:, c0:c0 + cn])
        z0 = (c0 // cn) * zc
        z_ref[:, z0:z0 + zc] = _dot(h_ref[...], wz_ref[:, z0:z0 + zc])
        taps = [cw_ref[8 * k:8 * k + 8, c0:c0 + cn][None] for k in range(CONV_W)]
        bias = cb_ref[:, c0:c0 + cn][None]
        for r0 in range(0, tm, rb):
            y = bias
            for k in range(CONV_W):
                y = y + taps[CONV_W - 1 - k] * pad_ref[HALO + r0 - k:HALO + r0 - k + rb,
                                                       c0:c0 + cn].reshape(rb // 8, 8, cn)
            y = _silu(y).reshape(rb, cn)
            if c0 < di:
                xs_ref[r0:r0 + rb, c0:c0 + cn] = y
            elif c0 < di + gn:
                bm_ref[r0:r0 + rb, c0 - di:c0 - di + cn] = y.astype(BF16)
            else:
                cm_ref[r0:r0 + rb, c0 - di - gn:c0 - di - gn + cn] = y.astype(BF16)
    pad_ref[0:HALO, :] = pad_ref[tm:tm + HALO, :]


def _mamba_in(x, g, wz, wxbc, wdt, conv_w, conv_b, dt_bias, seq, tm=512):
    t, d = x.shape
    di, n = wz.shape[1], wxbc.shape[1]
    gn = (n - di) // 2
    tm = min(tm, seq)
    row = lambda width: pl.BlockSpec((tm, width), lambda i: (i, 0))
    full = lambda a: pl.BlockSpec(a.shape, lambda i: (0, 0), pipeline_mode=pl.Buffered(1))
    taps = jnp.repeat(conv_w, 8, axis=0)
    bias = jnp.broadcast_to(conv_b.reshape(1, n), (8, n))
    consts = [g.reshape(1, d), wz, wxbc, wdt, taps, bias, dt_bias.reshape(1, LANES)]
    return pl.pallas_call(
        functools.partial(_mamba_in_kernel, tiles_per_seq=seq // tm),
        out_shape=[jax.ShapeDtypeStruct((t, di), F32), jax.ShapeDtypeStruct((t, di), F32),
                   jax.ShapeDtypeStruct((t, gn), BF16), jax.ShapeDtypeStruct((t, gn), BF16),
                   jax.ShapeDtypeStruct((t, LANES), F32)],
        grid=(t // tm,),
        in_specs=[row(d)] + [full(a) for a in consts],
        out_specs=[row(di), row(di), row(gn), row(gn), row(LANES)],
        scratch_shapes=[pltpu.VMEM((tm + HALO, n), F32), pltpu.VMEM((tm, d), BF16)],
        compiler_params=_params("arbitrary"),
        name="mamba_in",
    )(x, *consts)


def _ssd_kernel(xs_ref, bm_ref, cm_ref, dt_ref, z_ref, a_ref, d_ref, nw_ref, e2_ref, o_ref, st_ref, ex_ref):
    L = SSD_CHUNK
    n_heads = xs_ref.shape[1] // SSM_HEAD_DIM
    rep = n_heads // SSM_GROUPS
    gw = rep * SSM_HEAD_DIM

    @pl.when(pl.program_id(1) == 0)
    def _():
        st_ref[...] = jnp.zeros_like(st_ref)

    dt = dt_ref[...]
    da = dt * a_ref[...]
    li = lax.broadcasted_iota(jnp.int32, (L, L), 0)
    si = lax.broadcasted_iota(jnp.int32, (L, L), 1)
    causal = li >= si
    tril = causal.astype(BF16)
    hi = da.astype(BF16)
    r1 = da - hi.astype(F32)
    mid = r1.astype(BF16)
    lo = (r1 - mid.astype(F32)).astype(BF16)
    a_cs = _dot(tril, hi) + _dot(tril, mid) + _dot(tril, lo)
    a_cs_t = jnp.transpose(a_cs)
    total = a_cs[L - 1:L, :]
    ea = jnp.exp(a_cs)
    wdec = jnp.exp(total - a_cs) * dt
    cdec = jnp.exp(total)

    left = lax.broadcasted_iota(jnp.int32, (L, LANES), 1) < SSM_HEAD_DIM

    stack = jnp.concatenate([dt, wdec, ea, jnp.broadcast_to(cdec, (16, LANES))], axis=0)
    s_hi, s_lo = _split2(stack)
    ex_ref[...] = _dot(jnp.concatenate([s_hi, s_lo], axis=1), e2_ref[...])

    for g in range(SSM_GROUPS):
        cols = slice(g * gw, (g + 1) * gw)
        bg = bm_ref[:, g * D_STATE:(g + 1) * D_STATE]
        cg = cm_ref[:, g * D_STATE:(g + 1) * D_STATE]
        cb = _dot_nt(cg, bg)
        st = st_ref[g]
        y_off = _dot(cg, st.astype(BF16))
        xg = xs_ref[:, cols]
        xdt_g = xg * ex_ref[0:L, cols]
        ys = []
        for kk in range(rep // 2):
            xdt = xdt_g[:, kk * LANES:(kk + 1) * LANES]
            y_pair = jnp.zeros((L, LANES), F32)
            for side in range(2):
                h = g * rep + 2 * kk + side
                seg = a_cs[:, h:h + 1] - a_cs_t[h:h + 1, :]
                m = (cb * jnp.exp(jnp.where(causal, seg, -jnp.inf))).astype(BF16)
                xh = jnp.where(left if side == 0 else ~left, xdt, 0.0).astype(BF16)
                y_pair = y_pair + _dot(m, xh)
            ys.append(y_pair)
        y_g = jnp.concatenate(ys, axis=1) + y_off * ex_ref[2 * L:3 * L, cols]
        st_ref[g] = ex_ref[3 * L:3 * L + 1, cols] * st + _dot_tn(bg, (xg * ex_ref[L:2 * L, cols]).astype(BF16))
        zg = z_ref[:, g * gw:(g + 1) * gw]
        gy = (y_g + d_ref[:, g * gw:(g + 1) * gw] * xg) * _silu(zg)
        gy = gy * lax.rsqrt(jnp.mean(gy * gy, axis=-1, keepdims=True) + RMS_EPS)
        o_ref[:, g * gw:(g + 1) * gw] = (gy * nw_ref[:, g * gw:(g + 1) * gw]).astype(BF16)


def _ssd(xs, bm, cm, dt, z, a, d_exp, norm_w, bsz, seq):
    t, di = xs.shape
    nc = seq // SSD_CHUNK
    gw = di // SSM_GROUPS
    assert di % (SSM_GROUPS * LANES) == 0 and gw == di // SSM_GROUPS
    blk = lambda width: pl.BlockSpec((SSD_CHUNK, width), lambda b, c: (b * nc + c, 0))
    const = lambda width: pl.BlockSpec((1, width), lambda b, c: (0, 0))
    head_of = np.arange(di) // SSM_HEAD_DIM
    e2 = jnp.asarray(np.tile(np.arange(LANES)[:, None] == head_of[None, :], (2, 1)), dtype=BF16)
    return pl.pallas_call(
        _ssd_kernel,
        out_shape=jax.ShapeDtypeStruct((t, di), BF16),
        grid=(bsz, nc),
        in_specs=[blk(di), blk(bm.shape[1]), blk(cm.shape[1]), blk(LANES), blk(di),
                  const(LANES), const(di), const(di), pl.BlockSpec(e2.shape, lambda b, c: (0, 0))],
        out_specs=blk(di),
        scratch_shapes=[pltpu.VMEM((SSM_GROUPS, D_STATE, gw), F32),
                        pltpu.VMEM((3 * SSD_CHUNK + 16, di), F32)],
        compiler_params=_params("parallel", "arbitrary"),
        name="ssd",
    )(xs, bm, cm, dt, z, a, d_exp, norm_w, e2)


def _attn_layer(x, g, rope, w_in, w_out, sinks, ckp, ckw1, ckw2, cvp, cvw1, cvw2, bsz, seq):
    k, cmp_in, qt, vt, gates_t = _attn_in(x, g, rope, w_in, seq)
    o_a = _swa(qt, k, vt, sinks, bsz, seq)
    half = CMP_STRIDE * HEAD_DIM
    pos = jnp.stack([ckp, cvp]).reshape(2, 2, half)
    w1 = jnp.stack([ckw1, cvw1]).astype(BF16)
    w2 = jnp.stack([ckw2, cvw2]).astype(BF16)
    cmp_n, cmp_t = _compress(cmp_in, pos, w1, w2, bsz, seq)
    o_b = _nsa(qt, k, vt, cmp_n, cmp_t, gates_t, bsz, seq)
    w = w_out.astype(BF16)
    da = o_a.shape[1]
    return _proj_res(x, [o_a, o_b], [w[:da], w[da:]])


def _mamba_layer(x, g, w_in, conv_w, conv_b, dt_bias, a_log, d_skip, norm_w, w_out, bsz, seq):
    n_heads = dt_bias.shape[0]
    di = n_heads * SSM_HEAD_DIM
    gn = SSM_GROUPS * D_STATE
    w = w_in.astype(BF16)
    wz, wxbc, wdt = jnp.split(w, np.cumsum([di, di + 2 * gn]), axis=1)
    padh = LANES - n_heads
    z, xs, bm, cm, dt = _mamba_in(x, g, wz, wxbc, jnp.pad(wdt, ((0, 0), (0, padh))), conv_w, conv_b,
                                  jnp.pad(dt_bias, (0, padh)), seq)
    a = jnp.pad(-jnp.exp(a_log), (0, padh)).reshape(1, LANES)
    d_exp = jnp.repeat(d_skip, SSM_HEAD_DIM).reshape(1, di)
    yn = _ssd(xs, bm, cm, dt, z, a, d_exp, norm_w.reshape(1, di), bsz, seq)
    return _proj_res(x, [yn], [w_out.astype(BF16)])


def kernel(x, norm_gains, final_norm, ffn_w_gate, ffn_w_up, ffn_w_down, attn_w_in, attn_w_out, attn_sinks, cmp_k_pos, cmp_k_w1, cmp_k_w2, cmp_v_pos, cmp_v_w1, cmp_v_w2, ssm_w_in, ssm_conv_w, ssm_conv_b, ssm_dt_bias, ssm_a_log, ssm_d, ssm_norm, ssm_w_out):
    bsz, seq, d = x.shape
    depth = norm_gains.shape[0]
    rope = _rope_tables(seq)
    wg, wu, wd = (w.astype(BF16) for w in (ffn_w_gate, ffn_w_up, ffn_w_down))
    x = x.reshape(bsz * seq, d)
    for i in range(depth):
        g = norm_gains[i]
        x = _ffn(x, g[0], wg, wu, wd, i, 0)
        j = i // 2
        if i % 2 == 0:
            x = _attn_layer(x, g[1], rope, attn_w_in[j], attn_w_out[j], attn_sinks[j],
                            cmp_k_pos[j], cmp_k_w1[j], cmp_k_w2[j],
                            cmp_v_pos[j], cmp_v_w1[j], cmp_v_w2[j], bsz, seq)
        else:
            x = _mamba_layer(x, g[1], ssm_w_in[j], ssm_conv_w[j], ssm_conv_b[j], ssm_dt_bias[j],
                             ssm_a_log[j], ssm_d[j], ssm_norm[j], ssm_w_out[j], bsz, seq)
        x = _ffn(x, g[2], wg, wu, wd, i, 1, final_gain=final_norm if i == depth - 1 else None)
    return x.reshape(bsz, seq, d)
```

```python
import functools
import math

import numpy as np
import jax
import jax.numpy as jnp
from jax import lax
from jax.experimental import pallas as pl
from jax.experimental.pallas import tpu as pltpu

HEAD_DIM = 64
ROPE_THETA = 10000.0
Q_BLOCK = 128
SWA_Q_HEADS = 8
SWA_KV_HEADS = 2
SWA_WINDOW = 128
NSA_Q_HEADS = 8
NSA_KV_HEADS = 2
CMP_LEN = 32
CMP_STRIDE = 16
CMP_HIDDEN = 128
SEL_LEN = 64
SEL_TOPK = 16
NSA_WINDOW = 512
N_BRANCH = 3
SSM_HEAD_DIM = 64
SSM_GROUPS = 8
D_STATE = 128
CONV_W = 4
SSD_CHUNK = 128
RMS_EPS = 1e-6
NEG_INF = -1e30
SEL_FORCE = 1e4
LOG2E = math.log2(math.e)

LANES = 128
VMEM_LIMIT = 56 * 1024 * 1024
BF16 = jnp.bfloat16
F32 = jnp.float32


def _params(*sem, flags=None):
    return pltpu.CompilerParams(dimension_semantics=sem, vmem_limit_bytes=VMEM_LIMIT, flags=flags)


def _dot(a, b):
    return jnp.dot(a, b, preferred_element_type=F32)


def _dot_nt(a, b):
    return lax.dot_general(a, b, (((1,), (1,)), ((), ())), preferred_element_type=F32)


def _dot_tn(a, b):
    return lax.dot_general(a, b, (((0,), (0,)), ((), ())), preferred_element_type=F32)


def _split2(v):
    hi = v.astype(BF16)
    lo = (v - hi.astype(F32)).astype(BF16)
    return hi, lo


def _silu(x):
    hx = 0.5 * x
    return hx + hx * jnp.tanh(hx)


def _sigmoid(x):
    return 0.5 + 0.5 * jnp.tanh(0.5 * x)


def _rms(x, g):
    return x * lax.rsqrt(jnp.mean(x * x, axis=-1, keepdims=True) + RMS_EPS) * g


def _ffn_kernel(x_ref, g_ref, wg_ref, wu_ref, wd_ref, fg_ref, o_ref, *, final_norm, tf):
    x = x_ref[...]
    h = _rms(x, g_ref[...]).astype(BF16)
    acc = None
    for f0 in range(0, wg_ref.shape[1], tf):
        gate = _dot(h, wg_ref[:, f0:f0 + tf].astype(BF16))
        up = _dot(h, wu_ref[:, f0:f0 + tf].astype(BF16))
        part = _dot((_silu(gate) * up).astype(BF16), wd_ref[f0:f0 + tf, :].astype(BF16))
        acc = part if acc is None else acc + part
    y = x + 0.5 * acc
    if final_norm:
        y = _rms(y, fg_ref[...])
    o_ref[...] = y


def _ffn(x, g, wg, wu, wd, layer, slot, final_gain=None, tm=512, tf=256):
    t, d = x.shape
    f = wg.shape[-1]
    tm = min(tm, t)
    fg = jnp.ones((1, d), F32) if final_gain is None else final_gain.reshape(1, d)
    once = pl.Buffered(1)
    return pl.pallas_call(
        functools.partial(_ffn_kernel, final_norm=final_gain is not None, tf=tf),
        out_shape=jax.ShapeDtypeStruct((t, d), F32),
        grid=(t // tm,),
        in_specs=[
            pl.BlockSpec((tm, d), lambda i: (i, 0)),
            pl.BlockSpec((1, d), lambda i: (0, 0)),
            pl.BlockSpec((None, None, d, f), lambda i: (layer, slot, 0, 0), pipeline_mode=once),
            pl.BlockSpec((None, None, d, f), lambda i: (layer, slot, 0, 0), pipeline_mode=once),
            pl.BlockSpec((None, None, f, d), lambda i: (layer, slot, 0, 0), pipeline_mode=once),
            pl.BlockSpec((1, d), lambda i: (0, 0)),
        ],
        out_specs=pl.BlockSpec((tm, d), lambda i: (i, 0)),
        compiler_params=_params("parallel"),
        name="ffn",
    )(x, g.reshape(1, d), wg, wu, wd, fg)


def _proj_res_kernel(*refs, n_in):
    x_ref = refs[0]
    o_ref = refs[1 + 2 * n_in]
    y = x_ref[...]
    for i in range(n_in):
        y = y + _dot(refs[1 + i][...], refs[1 + n_in + i][...])
    o_ref[...] = y


def _proj_res(x, acts, ws, tm=512):
    t, d = x.shape
    tm = min(tm, t)
    n_in = len(acts)
    in_specs = [pl.BlockSpec((tm, d), lambda i: (i, 0))]
    in_specs += [pl.BlockSpec((tm, a.shape[1]), lambda i: (i, 0)) for a in acts]
    in_specs += [pl.BlockSpec(w.shape, lambda i: (0, 0)) for w in ws]
    return pl.pallas_call(
        functools.partial(_proj_res_kernel, n_in=n_in),
        out_shape=jax.ShapeDtypeStruct((t, d), F32),
        grid=(t // tm,),
        in_specs=in_specs,
        out_specs=pl.BlockSpec((tm, d), lambda i: (i, 0)),
        compiler_params=_params("parallel"),
        name="proj_res",
    )(x, *acts, *ws)


def _rope_lanes(y, c, s_lo, s_hi):
    return y * c + pltpu.roll(y, 96, 1) * s_lo + pltpu.roll(y, 32, 1) * s_hi


GATE_ROWS = 32
HALF = HEAD_DIM // 2


def _attn_in_kernel(x_ref, g_ref, c_ref, slo_ref, shi_ref, ct_ref, st_ref,
                    wk_ref, wcmp_ref, wqt_ref, wvt_ref, wgt_ref,
                    k_ref, cmp_ref, qt_ref, vt_ref, gt_ref):
    h = _rms(x_ref[...], g_ref[...]).astype(BF16)
    c, s_lo, s_hi = c_ref[...], slo_ref[...], shi_ref[...]
    yk = _dot(h, wk_ref[...])
    k_ref[...] = jnp.concatenate(
        [_rope_lanes(yk[:, i * LANES:(i + 1) * LANES], c, s_lo, s_hi) for i in range(3)], axis=1).astype(BF16)
    yc = _dot(h, wcmp_ref[...])
    cmp_ref[...] = jnp.concatenate([_rope_lanes(yc[:, :LANES], c, s_lo, s_hi), yc[:, LANES:]], axis=1)

    ct, st = ct_ref[...], st_ref[...]
    scale = HEAD_DIM ** -0.5 * LOG2E
    yq = _dot_nt(wqt_ref[...], h)
    for hd in range(yq.shape[0] // HEAD_DIM):
        lo = hd * HEAD_DIM
        y1, y2 = yq[lo:lo + HALF], yq[lo + HALF:lo + HEAD_DIM]
        qt_ref[lo:lo + HALF, :] = ((y1 * ct - y2 * st) * scale).astype(BF16)
        qt_ref[lo + HALF:lo + HEAD_DIM, :] = ((y2 * ct + y1 * st) * scale).astype(BF16)
    yv = _dot_nt(wvt_ref[...], h).astype(BF16)
    for cidx in range(yv.shape[1] // Q_BLOCK):
        vt_ref[cidx] = yv[:, cidx * Q_BLOCK:(cidx + 1) * Q_BLOCK]
    gt_ref[...] = _sigmoid(_dot_nt(wgt_ref[...], h))


def _rope_tables(seq):
    inv = 1.0 / (ROPE_THETA ** (jnp.arange(0, HEAD_DIM, 2, dtype=F32) / HEAD_DIM))
    ang = jnp.arange(seq, dtype=F32)[:, None] * inv[None, :]
    cos, sin = jnp.cos(ang), jnp.sin(ang)
    zero = jnp.zeros_like(sin)
    c = jnp.tile(cos, (1, 4))
    s_lo = jnp.tile(jnp.concatenate([-sin, zero], axis=1), (1, 2))
    s_hi = jnp.tile(jnp.concatenate([zero, sin], axis=1), (1, 2))
    return c, s_lo, s_hi, cos.T, sin.T


def _attn_in(x, g, rope, w_in, seq, tm=512):
    t, d = x.shape
    tm = min(tm, seq)
    dq = SWA_Q_HEADS * HEAD_DIM
    dkv = SWA_KV_HEADS * HEAD_DIM
    w = w_in.astype(BF16)
    cuts = np.cumsum([dq, dkv, dkv, dq] + [dkv] * 6)
    wqa, wka, wva, wqb, wkc, wvc, wks, wvs, wkw, wvw, wgt = jnp.split(w, cuts, axis=1)
    wk = jnp.concatenate([wka, wks, wkw], axis=1)
    wcmp = jnp.concatenate([wkc, wvc], axis=1)
    wqt = jnp.concatenate([wqa, wqb], axis=1).T
    wvt = jnp.concatenate([wva, wvs, wvw], axis=1).T
    wgt = jnp.pad(wgt, ((0, 0), (0, GATE_ROWS - wgt.shape[1]))).T
    ws = [wk, wcmp, wqt, wvt, wgt]
    n_pos = seq // tm
    nvt = tm // Q_BLOCK
    row = lambda i: (i, 0)
    col = lambda i: (0, i)
    return pl.pallas_call(
        _attn_in_kernel,
        out_shape=[
            jax.ShapeDtypeStruct((t, wk.shape[1]), BF16),
            jax.ShapeDtypeStruct((t, wcmp.shape[1]), F32),
            jax.ShapeDtypeStruct((wqt.shape[0], t), BF16),
            jax.ShapeDtypeStruct((t // Q_BLOCK, wvt.shape[0], Q_BLOCK), BF16),
            jax.ShapeDtypeStruct((GATE_ROWS, t), F32),
        ],
        grid=(t // tm,),
        in_specs=[pl.BlockSpec((tm, d), row), pl.BlockSpec((1, d), lambda i: (0, 0))]
        + [pl.BlockSpec((tm, LANES), lambda i: (i % n_pos, 0))] * 3
        + [pl.BlockSpec((HALF, tm), lambda i: (0, i % n_pos))] * 2
        + [pl.BlockSpec(wi.shape, lambda i: (0, 0)) for wi in ws],
        out_specs=[
            pl.BlockSpec((tm, wk.shape[1]), row),
            pl.BlockSpec((tm, wcmp.shape[1]), row),
            pl.BlockSpec((wqt.shape[0], tm), col),
            pl.BlockSpec((nvt, wvt.shape[0], Q_BLOCK), lambda i: (i, 0, 0)),
            pl.BlockSpec((GATE_ROWS, tm), col),
        ],
        compiler_params=_params("parallel"),
        name="attn_in",
    )(x, g.reshape(1, d), *rope, *ws)


def _group_q(qt_ref, g, rep):
    qs = jnp.concatenate(
        [qt_ref[(g * rep + r) * HEAD_DIM:(g * rep + r + 1) * HEAD_DIM, :] for r in range(rep)], axis=1)
    z = jnp.zeros_like(qs)
    return qs, jnp.concatenate([qs, z] if g == 0 else [z, qs], axis=0)


def _store_heads(o_ref, ot, g, rep):
    for p in range(rep // 2):
        pair = jnp.concatenate([ot[:, (2 * p) * Q_BLOCK:(2 * p + 1) * Q_BLOCK],
                                ot[:, (2 * p + 1) * Q_BLOCK:(2 * p + 2) * Q_BLOCK]], axis=0)
        c0 = (g * (rep // 2) + p) * LANES
        o_ref[:, c0:c0 + LANES] = jnp.transpose(pair).astype(BF16)


SUM_ROWS = 16


def _vt_tiles(vt_ref, tile0, ntiles, kind, g):
    lo = (kind * 2 + g) * HEAD_DIM
    vt = jnp.concatenate([vt_ref[tile0 + c, lo:lo + HEAD_DIM, :] for c in range(ntiles)], axis=1)
    return jnp.concatenate([vt, jnp.ones((SUM_ROWS, vt.shape[1]), BF16)], axis=0)


def _swa_kernel(sink_ref, qt_ref, k_ref, vt_ref, o_ref):
    n = pl.program_id(1)
    rep = SWA_Q_HEADS // SWA_KV_HEADS
    nq = rep * Q_BLOCK
    nkeys = 2 * Q_BLOCK
    tile0 = jnp.maximum(n - 1, 0)
    start = pl.multiple_of(tile0 * Q_BLOCK, Q_BLOCK)
    lane = lax.broadcasted_iota(jnp.int32, (nkeys, nq), 1)
    diff = n * Q_BLOCK + (lane & (Q_BLOCK - 1)) - start - lax.broadcasted_iota(jnp.int32, (nkeys, nq), 0)
    mask = diff.astype(jnp.uint32) < SWA_WINDOW
    head = lax.broadcasted_iota(jnp.int32, (1, nq), 1) // Q_BLOCK
    k2 = k_ref[pl.ds(start, nkeys), 0:LANES]
    for g in range(SWA_KV_HEADS):
        _, qs2 = _group_q(qt_ref, g, rep)
        s = jnp.where(mask, _dot(k2, qs2), NEG_INF)
        sink = jnp.zeros((1, nq), F32)
        for r in range(rep):
            sink = jnp.where(head == r, sink_ref[g * rep + r] * LOG2E, sink)
        m = jnp.maximum(jnp.max(s, axis=0, keepdims=True), sink)
        e = jnp.exp2(s - jnp.where(m > NEG_INF, m, 0.0))
        ot = _dot(_vt_tiles(vt_ref, tile0, 2, 0, g), e.astype(BF16))
        den = ot[HEAD_DIM:HEAD_DIM + 1] + jnp.exp2(sink - m)
        _store_heads(o_ref, ot[0:HEAD_DIM] / jnp.maximum(den, 1e-30), g, rep)


def _swa(qt, k, vt, sinks, bsz, seq):
    nq = seq // Q_BLOCK
    dq = SWA_Q_HEADS * HEAD_DIM
    return pl.pallas_call(
        _swa_kernel,
        out_shape=jax.ShapeDtypeStruct((bsz * seq, dq), BF16),
        grid=(bsz, nq),
        in_specs=[
            pl.BlockSpec(memory_space=pltpu.SMEM),
            pl.BlockSpec((dq, Q_BLOCK), lambda b, n: (0, b * nq + n)),
            pl.BlockSpec((seq, k.shape[1]), lambda b, n: (b, 0)),
            pl.BlockSpec((nq,) + vt.shape[1:], lambda b, n: (b, 0, 0)),
        ],
        out_specs=pl.BlockSpec((Q_BLOCK, dq), lambda b, n: (b * nq + n, 0)),
        compiler_params=_params("parallel", "arbitrary"),
        name="swa",
    )(sinks, qt, k, vt)


def _compress_kernel(r_ref, pos_ref, w1_ref, w2_ref, w2t_ref, o_ref, ot_ref):
    half = CMP_STRIDE * HEAD_DIM
    r = r_ref[...]
    top = _dot((r + pos_ref[0:1, :]).astype(BF16), w1_ref[0:half, :])
    bot = _dot((r + pos_ref[1:2, :]).astype(BF16), w1_ref[half:2 * half, :])
    pre = top + pltpu.roll(bot, bot.shape[0] - 1, 0)
    hid = jax.nn.gelu(pre).astype(BF16)
    o_ref[...] = _dot(hid, w2_ref[...]).astype(BF16)
    ot_ref[...] = _dot_nt(w2t_ref[...], hid).astype(BF16)


def _compress(cmp_in, pos, w1, w2, bsz, seq):
    ng = NSA_KV_HEADS
    nr = seq // CMP_STRIDE
    half = CMP_STRIDE * HEAD_DIM
    rows = cmp_in.reshape(bsz, nr, CMP_STRIDE, 2 * ng, HEAD_DIM).transpose(0, 3, 1, 2, 4)
    rows = rows.reshape(bsz, 2 * ng, nr, half)
    return pl.pallas_call(
        _compress_kernel,
        out_shape=[jax.ShapeDtypeStruct((bsz, 2 * ng, nr, HEAD_DIM), BF16),
                   jax.ShapeDtypeStruct((bsz, 2 * ng, HEAD_DIM, nr), BF16)],
        grid=(bsz, 2 * ng),
        in_specs=[
            pl.BlockSpec((None, None, nr, half), lambda b, j: (b, j, 0, 0)),
            pl.BlockSpec((None, 2, half), lambda b, j: (j // ng, 0, 0)),
            pl.BlockSpec((None, 2 * half, CMP_HIDDEN), lambda b, j: (j // ng, 0, 0)),
            pl.BlockSpec((None, CMP_HIDDEN, HEAD_DIM), lambda b, j: (j // ng, 0, 0)),
            pl.BlockSpec((None, HEAD_DIM, CMP_HIDDEN), lambda b, j: (j // ng, 0, 0)),
        ],
        out_specs=[pl.BlockSpec((None, None, nr, HEAD_DIM), lambda b, j: (b, j, 0, 0)),
                   pl.BlockSpec((None, None, HEAD_DIM, nr), lambda b, j: (b, j, 0, 0))],
        compiler_params=_params("parallel", "parallel"),
        name="compress",
    )(rows, pos, w1, w2, jnp.swapaxes(w2, 1, 2))


def _softmax_weights(s, mask):
    s = jnp.where(mask, s, NEG_INF)
    m = jnp.max(s, axis=0, keepdims=True)
    return jnp.exp2(s - jnp.where(m > NEG_INF, m, 0.0))


def _stable_topk_mask(score, k):
    nb = score.shape[0]
    jj = lax.broadcasted_iota(jnp.int32, (8, score.shape[1]), 0)
    rank = jnp.zeros(score.shape, F32)
    for j in range(nb):
        rj = score[j:j + 1, :]
        v0 = (j // 8) * 8
        mid = score[v0:v0 + 8]
        parts = [jnp.where((rj > mid) | ((rj == mid) & (jj + v0 > j)), 1.0, 0.0)]
        if v0 > 0:
            parts.insert(0, jnp.where(rj > score[0:v0], 1.0, 0.0))
        if v0 + 8 < nb:
            parts.append(jnp.where(rj >= score[v0 + 8:nb], 1.0, 0.0))
        rank = rank + (jnp.concatenate(parts, axis=0) if len(parts) > 1 else parts[0])
    return jnp.where(rank < k, 1.0, 0.0)


def _nsa_kernel(qt_ref, k_ref, vt_ref, cn_ref, ct_ref, gt_ref, wcs_ref, o_ref, sel_ref, part_ref,
                m_ref, acc_ref, *, seq, kt):
    n = pl.program_id(1)
    rep = NSA_Q_HEADS // NSA_KV_HEADS
    ng = NSA_KV_HEADS
    nq = rep * Q_BLOCK
    n_cmp = seq // CMP_STRIDE
    n_blk = seq // SEL_LEN
    bpt = kt // SEL_LEN
    tpk = kt // Q_BLOCK
    t0 = n * Q_BLOCK
    tq = t0 + (lax.broadcasted_iota(jnp.int32, (1, nq), 1) & (Q_BLOCK - 1))

    def gate(g, c):
        return jnp.concatenate(
            [gt_ref[(g * rep + r) * N_BRANCH + c:(g * rep + r) * N_BRANCH + c + 1, :] for r in range(rep)], axis=1)

    qs2s = []
    cmp_mask = lax.broadcasted_iota(jnp.int32, (n_cmp, nq), 0) * CMP_STRIDE + (CMP_LEN - 1) <= tq
    nwin = NSA_WINDOW + Q_BLOCK
    wt0 = jnp.maximum(n + 1 - nwin // Q_BLOCK, 0)
    start = pl.multiple_of(wt0 * Q_BLOCK, Q_BLOCK)
    diff = tq - start - lax.broadcasted_iota(jnp.int32, (nwin, nq), 0)
    win_mask = diff.astype(jnp.uint32) < NSA_WINDOW
    for g in range(ng):
        qs, qs2 = _group_q(qt_ref, g, rep)
        qs2s.append(qs2)
        e_cmp = _softmax_weights(_dot(cn_ref[g], qs), cmp_mask)
        p_cmp = e_cmp / jnp.maximum(jnp.sum(e_cmp, axis=0, keepdims=True), 1e-30)
        o_cmp = _dot(ct_ref[ng + g], p_cmp.astype(BF16))
        psum = p_cmp[:, 0:Q_BLOCK]
        for r in range(1, rep):
            psum = psum + p_cmp[:, r * Q_BLOCK:(r + 1) * Q_BLOCK]
        p_hi, p_lo = _split2(psum)
        imp = _dot(wcs_ref[...], p_hi) + _dot(wcs_ref[...], p_lo)
        jb = lax.broadcasted_iota(jnp.int32, (n_blk, Q_BLOCK), 0)
        cur = (t0 + lax.broadcasted_iota(jnp.int32, (n_blk, Q_BLOCK), 1)) // SEL_LEN
        valid = jb <= cur
        forced = valid & ((jb == 0) | (jb == cur) | (jb == cur - 1))
        score = imp + jnp.where(forced, SEL_FORCE, 0.0) - jnp.where(valid, 0.0, SEL_FORCE)
        sel = _stable_topk_mask(score, min(SEL_TOPK, n_blk))
        sel_ref[g] = jnp.concatenate([sel] * rep, axis=1)

        e_win = _softmax_weights(_dot(k_ref[pl.ds(start, nwin), 2 * LANES:3 * LANES], qs2), win_mask)
        o_win = _dot(_vt_tiles(vt_ref, wt0, nwin // Q_BLOCK, 2, g), e_win.astype(BF16))
        o_win = o_win[0:HEAD_DIM] / jnp.maximum(o_win[HEAD_DIM:HEAD_DIM + 1], 1e-30)
        part_ref[g] = gate(g, 0) * o_cmp + gate(g, 2) * o_win

    def scores(i):
        kblk = k_ref[pl.ds(pl.multiple_of(i * kt, kt), kt), LANES:2 * LANES]
        return tuple(_dot(kblk, qs2s[g]) for g in range(ng))

    def tile(i, s_all, carry, diagonal):
        if diagonal:
            causal = i * kt + lax.broadcasted_iota(jnp.int32, (kt, nq), 0) <= tq
        for g in range(ng):
            m = m_ref[g]
            s = s_all[g]
            selb = sel_ref[g, pl.ds(pl.multiple_of(i * bpt, bpt), bpt), :]
            s = jnp.concatenate(
                [jnp.where(selb[b:b + 1, :] > 0.5, s[b * SEL_LEN:(b + 1) * SEL_LEN, :], NEG_INF)
                 for b in range(bpt)], axis=0)
            if diagonal:
                s = jnp.where(causal, s, NEG_INF)
            m_new = jnp.maximum(m, jnp.max(s, axis=0, keepdims=True))
            e = jnp.exp2(s - jnp.where(m_new > NEG_INF, m_new, 0.0))
            acc_ref[g] = (jnp.exp2(m - m_new) * acc_ref[g]
                          + _dot(_vt_tiles(vt_ref, i * tpk, tpk, 1, g), e.astype(BF16)))
            m_ref[g] = m_new
        return carry

    m_ref[...] = jnp.full(m_ref.shape, NEG_INF, F32)
    acc_ref[...] = jnp.zeros(acc_ref.shape, F32)
    n_full = t0 // kt
    lax.fori_loop(0, n_full, lambda i, c: tile(i, scores(i), c, False), 0)
    tile(n_full, scores(n_full), 0, True)
    for g in range(ng):
        acc = acc_ref[g]
        o_slc = acc[0:HEAD_DIM] / jnp.maximum(acc[HEAD_DIM:HEAD_DIM + 1], 1e-30)
        _store_heads(o_ref, part_ref[g] + gate(g, 1) * o_slc, g, rep)


def _cmp_to_sel_weights_t(seq):
    n_cmp = (seq - CMP_LEN) // CMP_STRIDE + 1
    n_blk = seq // SEL_LEN
    cs = np.arange(n_cmp) * CMP_STRIDE
    ss = np.arange(n_blk) * SEL_LEN
    ov = np.minimum(cs[:, None] + CMP_LEN, ss[None, :] + SEL_LEN) - np.maximum(cs[:, None], ss[None, :])
    w = (np.clip(ov, 0, None) / CMP_LEN).astype(np.float32)
    w = np.concatenate([w, np.zeros((seq // CMP_STRIDE - n_cmp, n_blk), np.float32)], axis=0)
    return jnp.asarray(w.T, dtype=BF16)


def _nsa(qt, k, vt, cmp_n, cmp_t, gates_t, bsz, seq, kt=512):
    nq = seq // Q_BLOCK
    dq = NSA_Q_HEADS * HEAD_DIM
    kt = min(kt, seq)
    n_blk = seq // SEL_LEN
    assert NSA_WINDOW + Q_BLOCK <= seq and (kt // SEL_LEN) % 8 == 0
    return pl.pallas_call(
        functools.partial(_nsa_kernel, seq=seq, kt=kt),
        out_shape=jax.ShapeDtypeStruct((bsz * seq, dq), BF16),
        grid=(bsz, nq),
        in_specs=[
            pl.BlockSpec((dq, Q_BLOCK), lambda b, n: (1, b * nq + n)),
            pl.BlockSpec((seq, k.shape[1]), lambda b, n: (b, 0)),
            pl.BlockSpec((nq,) + vt.shape[1:], lambda b, n: (b, 0, 0)),
            pl.BlockSpec((None,) + cmp_n.shape[1:], lambda b, n: (b, 0, 0, 0)),
            pl.BlockSpec((None,) + cmp_t.shape[1:], lambda b, n: (b, 0, 0, 0)),
            pl.BlockSpec((GATE_ROWS, Q_BLOCK), lambda b, n: (0, b * nq + n)),
            pl.BlockSpec((n_blk, seq // CMP_STRIDE), lambda b, n: (0, 0)),
        ],
        out_specs=pl.BlockSpec((Q_BLOCK, dq), lambda b, n: (b * nq + n, 0)),
        scratch_shapes=[pltpu.VMEM((NSA_KV_HEADS, n_blk, (NSA_Q_HEADS // NSA_KV_HEADS) * Q_BLOCK), F32),
                        pltpu.VMEM((NSA_KV_HEADS, HEAD_DIM, (NSA_Q_HEADS // NSA_KV_HEADS) * Q_BLOCK), F32),
                        pltpu.VMEM((NSA_KV_HEADS, 1, (NSA_Q_HEADS // NSA_KV_HEADS) * Q_BLOCK), F32),
                        pltpu.VMEM((NSA_KV_HEADS, HEAD_DIM + SUM_ROWS,
                                    (NSA_Q_HEADS // NSA_KV_HEADS) * Q_BLOCK), F32)],
        compiler_params=_params("parallel", "arbitrary"),
        name="nsa",
    )(qt, k, vt, cmp_n, cmp_t, gates_t, _cmp_to_sel_weights_t(seq))


HALO = 8
CONV_COLS = 512
CONV_ROWS = 32


def _mamba_in_kernel(x_ref, g_ref, w_ref, wdt_ref, cw_ref, cb_ref, dtb_ref,
                     z_ref, xs_ref, bm_ref, cm_ref, dt_ref, pad_ref, h_ref, *, tiles_per_seq):
    tm = x_ref.shape[0]
    di = xs_ref.shape[1]
    gn = bm_ref.shape[1]
    n = pad_ref.shape[1]

    @pl.when(pl.program_id(0) % tiles_per_seq == 0)
    def _():
        pad_ref[0:HALO, :] = jnp.zeros((HALO, n), F32)

    h_ref[...] = _rms(x_ref[...], g_ref[...]).astype(BF16)
    dt_ref[...] = jax.nn.softplus(_dot(h_ref[...], wdt_ref[...]) + dtb_ref[...])
    zc = di * CONV_COLS // n
    cn, rb = CONV_COLS, CONV_ROWS

    def project(c0):
        pad_ref[HALO:HALO + tm, c0:c0 + cn] = _dot(h_ref[...], w_ref[:, di + c0:di + c0 + cn])
        z0 = (c0 // cn) * zc
        z_ref[:, z0:z0 + zc] = _dot(h_ref[...], w_ref[:, z0:z0 + zc])

    project(0)
    for c0 in range(0, n, cn):
        if c0 + cn < n:
            project(c0 + cn)
        taps = [cw_ref[8 * k:8 * k + 8, c0:c0 + cn][None] for k in range(CONV_W)]
        bias = cb_ref[:, c0:c0 + cn][None]
        for r0 in range(0, tm, rb):
            y = bias
            for k in range(CONV_W):
                y = y + taps[CONV_W - 1 - k] * pad_ref[HALO + r0 - k:HALO + r0 - k + rb,
                                                       c0:c0 + cn].reshape(rb // 8, 8, cn)
            y = _silu(y).reshape(rb, cn)
            if c0 < di:
                xs_ref[r0:r0 + rb, c0:c0 + cn] = y
            elif c0 < di + gn:
                bm_ref[r0:r0 + rb, c0 - di:c0 - di + cn] = y.astype(BF16)
            else:
                cm_ref[r0:r0 + rb, c0 - di - gn:c0 - di - gn + cn] = y.astype(BF16)
    pad_ref[0:HALO, :] = pad_ref[tm:tm + HALO, :]


def _mamba_in(x, g, w_in, layer, wdt, conv_w, conv_b, dt_bias, di, seq, tm=512):
    t, d = x.shape
    n = conv_w.shape[1]
    gn = (n - di) // 2
    tm = min(tm, seq)
    row = lambda width: pl.BlockSpec((tm, width), lambda i: (i, 0))
    once = pl.Buffered(1)
    full = lambda a: pl.BlockSpec(a.shape, lambda i: (0, 0), pipeline_mode=once)
    taps = jnp.repeat(conv_w, 8, axis=0)
    bias = jnp.broadcast_to(conv_b.reshape(1, n), (8, n))
    consts = [wdt, taps, bias, dt_bias.reshape(1, LANES)]
    return pl.pallas_call(
        functools.partial(_mamba_in_kernel, tiles_per_seq=seq // tm),
        out_shape=[jax.ShapeDtypeStruct((t, di), F32), jax.ShapeDtypeStruct((t, di), F32),
                   jax.ShapeDtypeStruct((t, gn), BF16), jax.ShapeDtypeStruct((t, gn), BF16),
                   jax.ShapeDtypeStruct((t, LANES), F32)],
        grid=(t // tm,),
        in_specs=[row(d), full(g.reshape(1, d)),
                  pl.BlockSpec((None,) + w_in.shape[1:], lambda i: (layer, 0, 0), pipeline_mode=once)]
        + [full(a) for a in consts],
        out_specs=[row(di), row(di), row(gn), row(gn), row(LANES)],
        scratch_shapes=[pltpu.VMEM((tm + HALO, n), F32), pltpu.VMEM((tm, d), BF16)],
        compiler_params=_params("arbitrary"),
        name="mamba_in",
    )(x, g.reshape(1, d), w_in, *consts)


def _ssd_kernel(xs_ref, bm_ref, cm_ref, dt_ref, z_ref, a_ref, d_ref, nw_ref, e2_ref, o_ref, st_ref, ex_ref):
    L = SSD_CHUNK
    n_heads = xs_ref.shape[1] // SSM_HEAD_DIM
    rep = n_heads // SSM_GROUPS
    gw = rep * SSM_HEAD_DIM

    @pl.when(pl.program_id(1) == 0)
    def _():
        st_ref[...] = jnp.zeros_like(st_ref)

    dt = dt_ref[...]
    da = dt * a_ref[...]
    li = lax.broadcasted_iota(jnp.int32, (L, L), 0)
    si = lax.broadcasted_iota(jnp.int32, (L, L), 1)
    causal = li >= si
    tril = causal.astype(BF16)
    hi = da.astype(BF16)
    r1 = da - hi.astype(F32)
    mid = r1.astype(BF16)
    lo = (r1 - mid.astype(F32)).astype(BF16)
    a_cs = _dot(tril, hi) + _dot(tril, mid) + _dot(tril, lo)
    a_cs_t = jnp.transpose(a_cs)
    total = a_cs[L - 1:L, :]
    ea = jnp.exp(a_cs)
    wdec = jnp.exp(total - a_cs) * dt
    cdec = jnp.exp(total)

    left = lax.broadcasted_iota(jnp.int32, (L, LANES), 1) < SSM_HEAD_DIM

    stack = jnp.concatenate([dt, wdec, ea, jnp.broadcast_to(cdec, (16, LANES))], axis=0)
    s_hi, s_lo = _split2(stack)
    ex_ref[...] = _dot(jnp.concatenate([s_hi, s_lo], axis=1), e2_ref[...])

    def group_dots(g):
        bg = bm_ref[:, g * D_STATE:(g + 1) * D_STATE]
        cg = cm_ref[:, g * D_STATE:(g + 1) * D_STATE]
        st = st_ref[g]
        return bg, _dot_nt(cg, bg), st, _dot(cg, st.astype(BF16))

    ahead = group_dots(0)
    for g in range(SSM_GROUPS):
        cols = slice(g * gw, (g + 1) * gw)
        bg, cb, st, y_off = ahead
        if g + 1 < SSM_GROUPS:
            ahead = group_dots(g + 1)
        xg = xs_ref[:, cols]
        xdt_g = xg * ex_ref[0:L, cols]
        ys = []
        for kk in range(rep // 2):
            xdt = xdt_g[:, kk * LANES:(kk + 1) * LANES]
            y_pair = jnp.zeros((L, LANES), F32)
            for side in range(2):
                h = g * rep + 2 * kk + side
                seg = a_cs[:, h:h + 1] - a_cs_t[h:h + 1, :]
                m = (cb * jnp.exp(jnp.where(causal, seg, -jnp.inf))).astype(BF16)
                xh = jnp.where(left if side == 0 else ~left, xdt, 0.0).astype(BF16)
                y_pair = y_pair + _dot(m, xh)
            ys.append(y_pair)
        y_g = jnp.concatenate(ys, axis=1) + y_off * ex_ref[2 * L:3 * L, cols]
        st_ref[g] = ex_ref[3 * L:3 * L + 1, cols] * st + _dot_tn(bg, (xg * ex_ref[L:2 * L, cols]).astype(BF16))
        zg = z_ref[:, g * gw:(g + 1) * gw]
        gy = (y_g + d_ref[:, g * gw:(g + 1) * gw] * xg) * _silu(zg)
        gy = gy * lax.rsqrt(jnp.mean(gy * gy, axis=-1, keepdims=True) + RMS_EPS)
        o_ref[:, g * gw:(g + 1) * gw] = (gy * nw_ref[:, g * gw:(g + 1) * gw]).astype(BF16)


def _ssd(xs, bm, cm, dt, z, a, d_exp, norm_w, bsz, seq):
    t, di = xs.shape
    nc = seq // SSD_CHUNK
    gw = di // SSM_GROUPS
    assert di % (SSM_GROUPS * LANES) == 0 and gw == di // SSM_GROUPS
    blk = lambda width: pl.BlockSpec((SSD_CHUNK, width), lambda b, c: (b * nc + c, 0))
    const = lambda width: pl.BlockSpec((1, width), lambda b, c: (0, 0))
    head_of = np.arange(di) // SSM_HEAD_DIM
    e2 = jnp.asarray(np.tile(np.arange(LANES)[:, None] == head_of[None, :], (2, 1)), dtype=BF16)
    return pl.pallas_call(
        _ssd_kernel,
        out_shape=jax.ShapeDtypeStruct((t, di), BF16),
        grid=(bsz, nc),
        in_specs=[blk(di), blk(bm.shape[1]), blk(cm.shape[1]), blk(LANES), blk(di),
                  const(LANES), const(di), const(di), pl.BlockSpec(e2.shape, lambda b, c: (0, 0))],
        out_specs=blk(di),
        scratch_shapes=[pltpu.VMEM((SSM_GROUPS, D_STATE, gw), F32),
                        pltpu.VMEM((3 * SSD_CHUNK + 16, di), F32)],
        compiler_params=_params("parallel", "arbitrary"),
        name="ssd",
    )(xs, bm, cm, dt, z, a, d_exp, norm_w, e2)


def _attn_layer(x, g, rope, w_in, w_out, sinks, ckp, ckw1, ckw2, cvp, cvw1, cvw2, bsz, seq):
    k, cmp_in, qt, vt, gates_t = _attn_in(x, g, rope, w_in, seq)
    o_a = _swa(qt, k, vt, sinks, bsz, seq)
    half = CMP_STRIDE * HEAD_DIM
    pos = jnp.stack([ckp, cvp]).reshape(2, 2, half)
    w1 = jnp.stack([ckw1, cvw1]).astype(BF16)
    w2 = jnp.stack([ckw2, cvw2]).astype(BF16)
    cmp_n, cmp_t = _compress(cmp_in, pos, w1, w2, bsz, seq)
    o_b = _nsa(qt, k, vt, cmp_n, cmp_t, gates_t, bsz, seq)
    w = w_out.astype(BF16)
    da = o_a.shape[1]
    return _proj_res(x, [o_a, o_b], [w[:da], w[da:]])


def _mamba_layer(x, g, w_in, w_in_f32, layer, conv_w, conv_b, dt_bias, a_log, d_skip, norm_w, w_out, bsz, seq):
    n_heads = dt_bias.shape[0]
    di = n_heads * SSM_HEAD_DIM
    gn = SSM_GROUPS * D_STATE
    padh = LANES - n_heads
    wdt = jnp.pad(w_in_f32[layer][:, 2 * di + 2 * gn:], ((0, 0), (0, padh))).astype(BF16)
    z, xs, bm, cm, dt = _mamba_in(x, g, w_in, layer, wdt, conv_w, conv_b, jnp.pad(dt_bias, (0, padh)), di, seq)
    a = jnp.pad(-jnp.exp(a_log), (0, padh)).reshape(1, LANES)
    d_exp = jnp.repeat(d_skip, SSM_HEAD_DIM).reshape(1, di)
    yn = _ssd(xs, bm, cm, dt, z, a, d_exp, norm_w.reshape(1, di), bsz, seq)
    return _proj_res(x, [yn], [w_out.astype(BF16)])


def kernel(x, norm_gains, final_norm, ffn_w_gate, ffn_w_up, ffn_w_down, attn_w_in, attn_w_out, attn_sinks, cmp_k_pos, cmp_k_w1, cmp_k_w2, cmp_v_pos, cmp_v_w1, cmp_v_w2, ssm_w_in, ssm_conv_w, ssm_conv_b, ssm_dt_bias, ssm_a_log, ssm_d, ssm_norm, ssm_w_out):
    bsz, seq, d = x.shape
    depth = norm_gains.shape[0]
    rope = _rope_tables(seq)
    wg, wu, wd = ffn_w_gate, ffn_w_up, ffn_w_down
    w_ssm = ssm_w_in.astype(BF16)
    x = x.reshape(bsz * seq, d)
    for i in range(depth):
        g = norm_gains[i]
        x = _ffn(x, g[0], wg, wu, wd, i, 0)
        j = i // 2
        if i % 2 == 0:
            x = _attn_layer(x, g[1], rope, attn_w_in[j], attn_w_out[j], attn_sinks[j],
                            cmp_k_pos[j], cmp_k_w1[j], cmp_k_w2[j],
                            cmp_v_pos[j], cmp_v_w1[j], cmp_v_w2[j], bsz, seq)
        else:
            x = _mamba_layer(x, g[1], w_ssm, ssm_w_in, j, ssm_conv_w[j], ssm_conv_b[j], ssm_dt_bias[j],
                             ssm_a_log[j], ssm_d[j], ssm_norm[j], ssm_w_out[j], bsz, seq)
        x = _ffn(x, g[2], wg, wu, wd, i, 1, final_gain=final_norm if i == depth - 1 else None)
    return x.reshape(bsz, seq, d)
```

```python
import functools
import math

import numpy as np
import jax
import jax.numpy as jnp
from jax import lax
from jax.experimental import pallas as pl
from jax.experimental.pallas import tpu as pltpu

HEAD_DIM = 64
ROPE_THETA = 10000.0
Q_BLOCK = 128
SWA_Q_HEADS = 8
SWA_KV_HEADS = 2
SWA_WINDOW = 128
NSA_Q_HEADS = 8
NSA_KV_HEADS = 2
CMP_LEN = 32
CMP_STRIDE = 16
CMP_HIDDEN = 128
SEL_LEN = 64
SEL_TOPK = 16
NSA_WINDOW = 512
N_BRANCH = 3
SSM_HEAD_DIM = 64
SSM_GROUPS = 8
D_STATE = 128
CONV_W = 4
SSD_CHUNK = 128
RMS_EPS = 1e-6
NEG_INF = -1e30
SEL_FORCE = 1e4
LOG2E = math.log2(math.e)

LANES = 128
VMEM_LIMIT = 56 * 1024 * 1024
BF16 = jnp.bfloat16
F32 = jnp.float32


def _params(*sem, flags=None):
    return pltpu.CompilerParams(dimension_semantics=sem, vmem_limit_bytes=VMEM_LIMIT, flags=flags)


def _dot(a, b):
    return jnp.dot(a, b, preferred_element_type=F32)


def _dot_nt(a, b):
    return lax.dot_general(a, b, (((1,), (1,)), ((), ())), preferred_element_type=F32)


def _dot_tn(a, b):
    return lax.dot_general(a, b, (((0,), (0,)), ((), ())), preferred_element_type=F32)


def _split2(v):
    hi = v.astype(BF16)
    lo = (v - hi.astype(F32)).astype(BF16)
    return hi, lo


def _silu(x):
    hx = 0.5 * x
    return hx + hx * jnp.tanh(hx)


def _sigmoid(x):
    return 0.5 + 0.5 * jnp.tanh(0.5 * x)


def _rms(x, g):
    return x * lax.rsqrt(jnp.mean(x * x, axis=-1, keepdims=True) + RMS_EPS) * g


def _ffn_kernel(x_ref, g_ref, wg_ref, wu_ref, wd_ref, fg_ref, o_ref, *, final_norm, tf):
    x = x_ref[...]
    h = _rms(x, g_ref[...]).astype(BF16)
    acc = None
    for f0 in range(0, wg_ref.shape[1], tf):
        gate = _dot(h, wg_ref[:, f0:f0 + tf].astype(BF16))
        up = _dot(h, wu_ref[:, f0:f0 + tf].astype(BF16))
        part = _dot((_silu(gate) * up).astype(BF16), wd_ref[f0:f0 + tf, :].astype(BF16))
        acc = part if acc is None else acc + part
    y = x + 0.5 * acc
    if final_norm:
        y = _rms(y, fg_ref[...])
    o_ref[...] = y


def _ffn(x, g, wg, wu, wd, layer, slot, final_gain=None, tm=512, tf=256):
    t, d = x.shape
    f = wg.shape[-1]
    tm = min(tm, t)
    fg = jnp.ones((1, d), F32) if final_gain is None else final_gain.reshape(1, d)
    once = pl.Buffered(1)
    return pl.pallas_call(
        functools.partial(_ffn_kernel, final_norm=final_gain is not None, tf=tf),
        out_shape=jax.ShapeDtypeStruct((t, d), F32),
        grid=(t // tm,),
        in_specs=[
            pl.BlockSpec((tm, d), lambda i: (i, 0)),
            pl.BlockSpec((1, d), lambda i: (0, 0)),
            pl.BlockSpec((None, None, d, f), lambda i: (layer, slot, 0, 0), pipeline_mode=once),
            pl.BlockSpec((None, None, d, f), lambda i: (layer, slot, 0, 0), pipeline_mode=once),
            pl.BlockSpec((None, None, f, d), lambda i: (layer, slot, 0, 0), pipeline_mode=once),
            pl.BlockSpec((1, d), lambda i: (0, 0)),
        ],
        out_specs=pl.BlockSpec((tm, d), lambda i: (i, 0)),
        compiler_params=_params("parallel"),
        name="ffn",
    )(x, g.reshape(1, d), wg, wu, wd, fg)


def _proj_res_kernel(*refs, n_in):
    x_ref = refs[0]
    o_ref = refs[1 + 2 * n_in]
    y = x_ref[...]
    for i in range(n_in):
        y = y + _dot(refs[1 + i][...], refs[1 + n_in + i][...])
    o_ref[...] = y


def _proj_res(x, acts, ws, tm=1024):
    t, d = x.shape
    tm = min(tm, t)
    n_in = len(acts)
    in_specs = [pl.BlockSpec((tm, d), lambda i: (i, 0))]
    in_specs += [pl.BlockSpec((tm, a.shape[1]), lambda i: (i, 0)) for a in acts]
    in_specs += [pl.BlockSpec(w.shape, lambda i: (0, 0)) for w in ws]
    return pl.pallas_call(
        functools.partial(_proj_res_kernel, n_in=n_in),
        out_shape=jax.ShapeDtypeStruct((t, d), F32),
        grid=(t // tm,),
        in_specs=in_specs,
        out_specs=pl.BlockSpec((tm, d), lambda i: (i, 0)),
        compiler_params=_params("parallel"),
        name="proj_res",
    )(x, *acts, *ws)


def _rope_lanes(y, c, s_lo, s_hi):
    return y * c + pltpu.roll(y, 96, 1) * s_lo + pltpu.roll(y, 32, 1) * s_hi


GATE_ROWS = 32
HALF = HEAD_DIM // 2


def _attn_in_kernel(x_ref, g_ref, c_ref, slo_ref, shi_ref, ct_ref, st_ref,
                    wk_ref, wcmp_ref, wqt_ref, wvt_ref, wgt_ref,
                    k_ref, cmp_ref, qt_ref, vt_ref, gt_ref):
    h = _rms(x_ref[...], g_ref[...]).astype(BF16)
    c, s_lo, s_hi = c_ref[...], slo_ref[...], shi_ref[...]
    yk = _dot(h, wk_ref[...])
    k_ref[...] = jnp.concatenate(
        [_rope_lanes(yk[:, i * LANES:(i + 1) * LANES], c, s_lo, s_hi) for i in range(3)], axis=1).astype(BF16)
    yc = _dot(h, wcmp_ref[...])
    cmp_ref[...] = jnp.concatenate([_rope_lanes(yc[:, :LANES], c, s_lo, s_hi), yc[:, LANES:]], axis=1)

    ct, st = ct_ref[...], st_ref[...]
    scale = HEAD_DIM ** -0.5 * LOG2E
    yq = _dot_nt(wqt_ref[...], h)
    for hd in range(yq.shape[0] // HEAD_DIM):
        lo = hd * HEAD_DIM
        y1, y2 = yq[lo:lo + HALF], yq[lo + HALF:lo + HEAD_DIM]
        qt_ref[lo:lo + HALF, :] = ((y1 * ct - y2 * st) * scale).astype(BF16)
        qt_ref[lo + HALF:lo + HEAD_DIM, :] = ((y2 * ct + y1 * st) * scale).astype(BF16)
    yv = _dot_nt(wvt_ref[...], h).astype(BF16)
    for cidx in range(yv.shape[1] // Q_BLOCK):
        vt_ref[cidx] = yv[:, cidx * Q_BLOCK:(cidx + 1) * Q_BLOCK]
    gt_ref[...] = _sigmoid(_dot_nt(wgt_ref[...], h))


def _rope_tables(seq):
    inv = 1.0 / (ROPE_THETA ** (jnp.arange(0, HEAD_DIM, 2, dtype=F32) / HEAD_DIM))
    ang = jnp.arange(seq, dtype=F32)[:, None] * inv[None, :]
    cos, sin = jnp.cos(ang), jnp.sin(ang)
    zero = jnp.zeros_like(sin)
    c = jnp.tile(cos, (1, 4))
    s_lo = jnp.tile(jnp.concatenate([-sin, zero], axis=1), (1, 2))
    s_hi = jnp.tile(jnp.concatenate([zero, sin], axis=1), (1, 2))
    return c, s_lo, s_hi, cos.T, sin.T


def _attn_in(x, g, rope, w_in, seq, tm=512):
    t, d = x.shape
    tm = min(tm, seq)
    dq = SWA_Q_HEADS * HEAD_DIM
    dkv = SWA_KV_HEADS * HEAD_DIM
    w = w_in.astype(BF16)
    cuts = np.cumsum([dq, dkv, dkv, dq] + [dkv] * 6)
    wqa, wka, wva, wqb, wkc, wvc, wks, wvs, wkw, wvw, wgt = jnp.split(w, cuts, axis=1)
    wk = jnp.concatenate([wka, wks, wkw], axis=1)
    wcmp = jnp.concatenate([wkc, wvc], axis=1)
    wqt = jnp.concatenate([wqa, wqb], axis=1).T
    wvt = jnp.concatenate([wva, wvs, wvw], axis=1).T
    wgt = jnp.pad(wgt, ((0, 0), (0, GATE_ROWS - wgt.shape[1]))).T
    ws = [wk, wcmp, wqt, wvt, wgt]
    n_pos = seq // tm
    nvt = tm // Q_BLOCK
    row = lambda i: (i, 0)
    col = lambda i: (0, i)
    return pl.pallas_call(
        _attn_in_kernel,
        out_shape=[
            jax.ShapeDtypeStruct((t, wk.shape[1]), BF16),
            jax.ShapeDtypeStruct((t, wcmp.shape[1]), F32),
            jax.ShapeDtypeStruct((wqt.shape[0], t), BF16),
            jax.ShapeDtypeStruct((t // Q_BLOCK, wvt.shape[0], Q_BLOCK), BF16),
            jax.ShapeDtypeStruct((GATE_ROWS, t), F32),
        ],
        grid=(t // tm,),
        in_specs=[pl.BlockSpec((tm, d), row), pl.BlockSpec((1, d), lambda i: (0, 0))]
        + [pl.BlockSpec((tm, LANES), lambda i: (i % n_pos, 0))] * 3
        + [pl.BlockSpec((HALF, tm), lambda i: (0, i % n_pos))] * 2
        + [pl.BlockSpec(wi.shape, lambda i: (0, 0)) for wi in ws],
        out_specs=[
            pl.BlockSpec((tm, wk.shape[1]), row),
            pl.BlockSpec((tm, wcmp.shape[1]), row),
            pl.BlockSpec((wqt.shape[0], tm), col),
            pl.BlockSpec((nvt, wvt.shape[0], Q_BLOCK), lambda i: (i, 0, 0)),
            pl.BlockSpec((GATE_ROWS, tm), col),
        ],
        compiler_params=_params("parallel"),
        name="attn_in",
    )(x, g.reshape(1, d), *rope, *ws)


def _group_q(qt_ref, g, rep):
    qs = jnp.concatenate(
        [qt_ref[(g * rep + r) * HEAD_DIM:(g * rep + r + 1) * HEAD_DIM, :] for r in range(rep)], axis=1)
    z = jnp.zeros_like(qs)
    return qs, jnp.concatenate([qs, z] if g == 0 else [z, qs], axis=0)


def _store_heads(o_ref, ot, g, rep):
    for p in range(rep // 2):
        pair = jnp.concatenate([ot[:, (2 * p) * Q_BLOCK:(2 * p + 1) * Q_BLOCK],
                                ot[:, (2 * p + 1) * Q_BLOCK:(2 * p + 2) * Q_BLOCK]], axis=0)
        c0 = (g * (rep // 2) + p) * LANES
        o_ref[:, c0:c0 + LANES] = jnp.transpose(pair).astype(BF16)


SUM_ROWS = 16


def _vt_tiles(vt_ref, tile0, ntiles, kind, g):
    lo = (kind * 2 + g) * HEAD_DIM
    vt = jnp.concatenate([vt_ref[tile0 + c, lo:lo + HEAD_DIM, :] for c in range(ntiles)], axis=1)
    return jnp.concatenate([vt, jnp.ones((SUM_ROWS, vt.shape[1]), BF16)], axis=0)


def _swa_kernel(sink_ref, qt_ref, k_ref, vt_ref, o_ref):
    n = pl.program_id(1)
    rep = SWA_Q_HEADS // SWA_KV_HEADS
    nq = rep * Q_BLOCK
    nkeys = 2 * Q_BLOCK
    tile0 = jnp.maximum(n - 1, 0)
    start = pl.multiple_of(tile0 * Q_BLOCK, Q_BLOCK)
    lane = lax.broadcasted_iota(jnp.int32, (nkeys, nq), 1)
    diff = n * Q_BLOCK + (lane & (Q_BLOCK - 1)) - start - lax.broadcasted_iota(jnp.int32, (nkeys, nq), 0)
    mask = diff.astype(jnp.uint32) < SWA_WINDOW
    head = lax.broadcasted_iota(jnp.int32, (1, nq), 1) // Q_BLOCK
    k2 = k_ref[pl.ds(start, nkeys), 0:LANES]
    scores = [_dot(k2, _group_q(qt_ref, g, rep)[1]) for g in range(SWA_KV_HEADS)]
    for g in range(SWA_KV_HEADS):
        s = jnp.where(mask, scores[g], NEG_INF)
        sink = jnp.zeros((1, nq), F32)
        for r in range(rep):
            sink = jnp.where(head == r, sink_ref[g * rep + r] * LOG2E, sink)
        m = jnp.maximum(jnp.max(s, axis=0, keepdims=True), sink)
        e = jnp.exp2(s - jnp.where(m > NEG_INF, m, 0.0))
        ot = _dot(_vt_tiles(vt_ref, tile0, 2, 0, g), e.astype(BF16))
        den = ot[HEAD_DIM:HEAD_DIM + 1] + jnp.exp2(sink - m)
        _store_heads(o_ref, ot[0:HEAD_DIM] / jnp.maximum(den, 1e-30), g, rep)


def _swa(qt, k, vt, sinks, bsz, seq):
    nq = seq // Q_BLOCK
    dq = SWA_Q_HEADS * HEAD_DIM
    return pl.pallas_call(
        _swa_kernel,
        out_shape=jax.ShapeDtypeStruct((bsz * seq, dq), BF16),
        grid=(bsz, nq),
        in_specs=[
            pl.BlockSpec(memory_space=pltpu.SMEM),
            pl.BlockSpec((dq, Q_BLOCK), lambda b, n: (0, b * nq + n)),
            pl.BlockSpec((seq, k.shape[1]), lambda b, n: (b, 0)),
            pl.BlockSpec((nq,) + vt.shape[1:], lambda b, n: (b, 0, 0)),
        ],
        out_specs=pl.BlockSpec((Q_BLOCK, dq), lambda b, n: (b * nq + n, 0)),
        compiler_params=_params("parallel", "arbitrary"),
        name="swa",
    )(sinks, qt, k, vt)


def _compress_kernel(r_ref, pos_ref, w1_ref, w2_ref, w2t_ref, o_ref, ot_ref):
    half = CMP_STRIDE * HEAD_DIM
    r = r_ref[...]
    top = _dot((r + pos_ref[0:1, :]).astype(BF16), w1_ref[0:half, :])
    bot = _dot((r + pos_ref[1:2, :]).astype(BF16), w1_ref[half:2 * half, :])
    pre = top + pltpu.roll(bot, bot.shape[0] - 1, 0)
    hid = jax.nn.gelu(pre).astype(BF16)
    o_ref[...] = _dot(hid, w2_ref[...]).astype(BF16)
    ot_ref[...] = _dot_nt(w2t_ref[...], hid).astype(BF16)


def _compress(cmp_in, pos, w1, w2, bsz, seq):
    ng = NSA_KV_HEADS
    nr = seq // CMP_STRIDE
    half = CMP_STRIDE * HEAD_DIM
    rows = cmp_in.reshape(bsz, nr, CMP_STRIDE, 2 * ng, HEAD_DIM).transpose(0, 3, 1, 2, 4)
    rows = rows.reshape(bsz, 2 * ng, nr, half)
    return pl.pallas_call(
        _compress_kernel,
        out_shape=[jax.ShapeDtypeStruct((bsz, 2 * ng, nr, HEAD_DIM), BF16),
                   jax.ShapeDtypeStruct((bsz, 2 * ng, HEAD_DIM, nr), BF16)],
        grid=(bsz, 2 * ng),
        in_specs=[
            pl.BlockSpec((None, None, nr, half), lambda b, j: (b, j, 0, 0)),
            pl.BlockSpec((None, 2, half), lambda b, j: (j // ng, 0, 0)),
            pl.BlockSpec((None, 2 * half, CMP_HIDDEN), lambda b, j: (j // ng, 0, 0)),
            pl.BlockSpec((None, CMP_HIDDEN, HEAD_DIM), lambda b, j: (j // ng, 0, 0)),
            pl.BlockSpec((None, HEAD_DIM, CMP_HIDDEN), lambda b, j: (j // ng, 0, 0)),
        ],
        out_specs=[pl.BlockSpec((None, None, nr, HEAD_DIM), lambda b, j: (b, j, 0, 0)),
                   pl.BlockSpec((None, None, HEAD_DIM, nr), lambda b, j: (b, j, 0, 0))],
        compiler_params=_params("parallel", "parallel"),
        name="compress",
    )(rows, pos, w1, w2, jnp.swapaxes(w2, 1, 2))


def _softmax_weights(s, mask):
    s = jnp.where(mask, s, NEG_INF)
    m = jnp.max(s, axis=0, keepdims=True)
    return jnp.exp2(s - jnp.where(m > NEG_INF, m, 0.0))


def _stable_topk_mask(score, k):
    nb = score.shape[0]
    jj = lax.broadcasted_iota(jnp.int32, (8, score.shape[1]), 0)
    rank = jnp.zeros(score.shape, F32)
    for j in range(nb):
        rj = score[j:j + 1, :]
        v0 = (j // 8) * 8
        mid = score[v0:v0 + 8]
        parts = [jnp.where((rj > mid) | ((rj == mid) & (jj + v0 > j)), 1.0, 0.0)]
        if v0 > 0:
            parts.insert(0, jnp.where(rj > score[0:v0], 1.0, 0.0))
        if v0 + 8 < nb:
            parts.append(jnp.where(rj >= score[v0 + 8:nb], 1.0, 0.0))
        rank = rank + (jnp.concatenate(parts, axis=0) if len(parts) > 1 else parts[0])
    return jnp.where(rank < k, 1.0, 0.0)


def _nsa_kernel(qt_ref, k_ref, vt_ref, cn_ref, ct_ref, gt_ref, wcs_ref, o_ref, sel_ref, part_ref,
                m_ref, acc_ref, *, seq, kt):
    n = pl.program_id(1)
    rep = NSA_Q_HEADS // NSA_KV_HEADS
    ng = NSA_KV_HEADS
    nq = rep * Q_BLOCK
    n_cmp = seq // CMP_STRIDE
    n_blk = seq // SEL_LEN
    bpt = kt // SEL_LEN
    tpk = kt // Q_BLOCK
    t0 = n * Q_BLOCK
    tq = t0 + (lax.broadcasted_iota(jnp.int32, (1, nq), 1) & (Q_BLOCK - 1))

    def gate(g, c):
        return jnp.concatenate(
            [gt_ref[(g * rep + r) * N_BRANCH + c:(g * rep + r) * N_BRANCH + c + 1, :] for r in range(rep)], axis=1)

    qs2s = []
    cmp_mask = lax.broadcasted_iota(jnp.int32, (n_cmp, nq), 0) * CMP_STRIDE + (CMP_LEN - 1) <= tq
    nwin = NSA_WINDOW + Q_BLOCK
    wt0 = jnp.maximum(n + 1 - nwin // Q_BLOCK, 0)
    start = pl.multiple_of(wt0 * Q_BLOCK, Q_BLOCK)
    diff = tq - start - lax.broadcasted_iota(jnp.int32, (nwin, nq), 0)
    win_mask = diff.astype(jnp.uint32) < NSA_WINDOW
    jb = lax.broadcasted_iota(jnp.int32, (n_blk, Q_BLOCK), 0)
    cur = (t0 + lax.broadcasted_iota(jnp.int32, (n_blk, Q_BLOCK), 1)) // SEL_LEN
    valid = jb <= cur
    forced = valid & ((jb == 0) | (jb == cur) | (jb == cur - 1))
    kwin = k_ref[pl.ds(start, nwin), 2 * LANES:3 * LANES]
    s_cmp, s_win = [], []
    for g in range(ng):
        qs, qs2 = _group_q(qt_ref, g, rep)
        qs2s.append(qs2)
        s_cmp.append(_dot(cn_ref[g], qs))
        s_win.append(_dot(kwin, qs2))
    for g in range(ng):
        e_cmp = _softmax_weights(s_cmp[g], cmp_mask)
        p_cmp = e_cmp / jnp.maximum(jnp.sum(e_cmp, axis=0, keepdims=True), 1e-30)
        o_cmp = _dot(ct_ref[ng + g], p_cmp.astype(BF16))
        psum = p_cmp[:, 0:Q_BLOCK]
        for r in range(1, rep):
            psum = psum + p_cmp[:, r * Q_BLOCK:(r + 1) * Q_BLOCK]
        p_hi, p_lo = _split2(psum)
        imp = _dot(wcs_ref[...], p_hi) + _dot(wcs_ref[...], p_lo)
        score = imp + jnp.where(forced, SEL_FORCE, 0.0) - jnp.where(valid, 0.0, SEL_FORCE)
        sel = _stable_topk_mask(score, min(SEL_TOPK, n_blk))
        sel_ref[g] = jnp.concatenate([sel] * rep, axis=1)

        e_win = _softmax_weights(s_win[g], win_mask)
        o_win = _dot(_vt_tiles(vt_ref, wt0, nwin // Q_BLOCK, 2, g), e_win.astype(BF16))
        o_win = o_win[0:HEAD_DIM] / jnp.maximum(o_win[HEAD_DIM:HEAD_DIM + 1], 1e-30)
        part_ref[g] = gate(g, 0) * o_cmp + gate(g, 2) * o_win

    def scores(i):
        kblk = k_ref[pl.ds(pl.multiple_of(i * kt, kt), kt), LANES:2 * LANES]
        return tuple(_dot(kblk, qs2s[g]) for g in range(ng))

    def tile(i, s_all, carry, diagonal):
        if diagonal:
            causal = i * kt + lax.broadcasted_iota(jnp.int32, (kt, nq), 0) <= tq
        for g in range(ng):
            m = m_ref[g]
            s = s_all[g]
            selb = sel_ref[g, pl.ds(pl.multiple_of(i * bpt, bpt), bpt), :]
            s = jnp.concatenate(
                [jnp.where(selb[b:b + 1, :] > 0.5, s[b * SEL_LEN:(b + 1) * SEL_LEN, :], NEG_INF)
                 for b in range(bpt)], axis=0)
            if diagonal:
                s = jnp.where(causal, s, NEG_INF)
            m_new = jnp.maximum(m, jnp.max(s, axis=0, keepdims=True))
            e = jnp.exp2(s - jnp.where(m_new > NEG_INF, m_new, 0.0))
            acc_ref[g] = (jnp.exp2(m - m_new) * acc_ref[g]
                          + _dot(_vt_tiles(vt_ref, i * tpk, tpk, 1, g), e.astype(BF16)))
            m_ref[g] = m_new
        return carry

    m_ref[...] = jnp.full(m_ref.shape, NEG_INF, F32)
    acc_ref[...] = jnp.zeros(acc_ref.shape, F32)
    n_full = t0 // kt
    lax.fori_loop(0, n_full, lambda i, c: tile(i, scores(i), c, False), 0)
    tile(n_full, scores(n_full), 0, True)
    for g in range(ng):
        acc = acc_ref[g]
        o_slc = acc[0:HEAD_DIM] / jnp.maximum(acc[HEAD_DIM:HEAD_DIM + 1], 1e-30)
        _store_heads(o_ref, part_ref[g] + gate(g, 1) * o_slc, g, rep)


def _cmp_to_sel_weights_t(seq):
    n_cmp = (seq - CMP_LEN) // CMP_STRIDE + 1
    n_blk = seq // SEL_LEN
    cs = np.arange(n_cmp) * CMP_STRIDE
    ss = np.arange(n_blk) * SEL_LEN
    ov = np.minimum(cs[:, None] + CMP_LEN, ss[None, :] + SEL_LEN) - np.maximum(cs[:, None], ss[None, :])
    w = (np.clip(ov, 0, None) / CMP_LEN).astype(np.float32)
    w = np.concatenate([w, np.zeros((seq // CMP_STRIDE - n_cmp, n_blk), np.float32)], axis=0)
    return jnp.asarray(w.T, dtype=BF16)


def _nsa(qt, k, vt, cmp_n, cmp_t, gates_t, bsz, seq, kt=512):
    nq = seq // Q_BLOCK
    dq = NSA_Q_HEADS * HEAD_DIM
    kt = min(kt, seq)
    n_blk = seq // SEL_LEN
    assert NSA_WINDOW + Q_BLOCK <= seq and (kt // SEL_LEN) % 8 == 0
    return pl.pallas_call(
        functools.partial(_nsa_kernel, seq=seq, kt=kt),
        out_shape=jax.ShapeDtypeStruct((bsz * seq, dq), BF16),
        grid=(bsz, nq),
        in_specs=[
            pl.BlockSpec((dq, Q_BLOCK), lambda b, n: (1, b * nq + n)),
            pl.BlockSpec((seq, k.shape[1]), lambda b, n: (b, 0)),
            pl.BlockSpec((nq,) + vt.shape[1:], lambda b, n: (b, 0, 0)),
            pl.BlockSpec((None,) + cmp_n.shape[1:], lambda b, n: (b, 0, 0, 0)),
            pl.BlockSpec((None,) + cmp_t.shape[1:], lambda b, n: (b, 0, 0, 0)),
            pl.BlockSpec((GATE_ROWS, Q_BLOCK), lambda b, n: (0, b * nq + n)),
            pl.BlockSpec((n_blk, seq // CMP_STRIDE), lambda b, n: (0, 0)),
        ],
        out_specs=pl.BlockSpec((Q_BLOCK, dq), lambda b, n: (b * nq + n, 0)),
        scratch_shapes=[pltpu.VMEM((NSA_KV_HEADS, n_blk, (NSA_Q_HEADS // NSA_KV_HEADS) * Q_BLOCK), F32),
                        pltpu.VMEM((NSA_KV_HEADS, HEAD_DIM, (NSA_Q_HEADS // NSA_KV_HEADS) * Q_BLOCK), F32),
                        pltpu.VMEM((NSA_KV_HEADS, 1, (NSA_Q_HEADS // NSA_KV_HEADS) * Q_BLOCK), F32),
                        pltpu.VMEM((NSA_KV_HEADS, HEAD_DIM + SUM_ROWS,
                                    (NSA_Q_HEADS // NSA_KV_HEADS) * Q_BLOCK), F32)],
        compiler_params=_params("parallel", "arbitrary"),
        name="nsa",
    )(qt, k, vt, cmp_n, cmp_t, gates_t, _cmp_to_sel_weights_t(seq))


HALO = 8
CONV_COLS = 512
CONV_ROWS = 32


def _mamba_in_kernel(x_ref, g_ref, w_ref, wdt_ref, cw_ref, cb_ref, dtb_ref,
                     z_ref, xs_ref, bm_ref, cm_ref, dt_ref, pad_ref, h_ref, *, tiles_per_seq):
    tm = x_ref.shape[0]
    di = xs_ref.shape[1]
    gn = bm_ref.shape[1]
    n = pad_ref.shape[1]

    @pl.when(pl.program_id(0) % tiles_per_seq == 0)
    def _():
        pad_ref[0:HALO, :] = jnp.zeros((HALO, n), F32)

    h_ref[...] = _rms(x_ref[...], g_ref[...]).astype(BF16)
    dt_ref[...] = jax.nn.softplus(_dot(h_ref[...], wdt_ref[...]) + dtb_ref[...])
    zc = di * CONV_COLS // n
    cn, rb = CONV_COLS, CONV_ROWS

    def project(c0):
        pad_ref[HALO:HALO + tm, c0:c0 + cn] = _dot(h_ref[...], w_ref[:, di + c0:di + c0 + cn])
        z0 = (c0 // cn) * zc
        z_ref[:, z0:z0 + zc] = _dot(h_ref[...], w_ref[:, z0:z0 + zc])

    project(0)
    for c0 in range(0, n, cn):
        if c0 + cn < n:
            project(c0 + cn)
        taps = [cw_ref[8 * k:8 * k + 8, c0:c0 + cn][None] for k in range(CONV_W)]
        bias = cb_ref[:, c0:c0 + cn][None]
        for r0 in range(0, tm, rb):
            y = bias
            for k in range(CONV_W):
                y = y + taps[CONV_W - 1 - k] * pad_ref[HALO + r0 - k:HALO + r0 - k + rb,
                                                       c0:c0 + cn].reshape(rb // 8, 8, cn)
            y = _silu(y).reshape(rb, cn)
            if c0 < di:
                xs_ref[r0:r0 + rb, c0:c0 + cn] = y
            elif c0 < di + gn:
                bm_ref[r0:r0 + rb, c0 - di:c0 - di + cn] = y.astype(BF16)
            else:
                cm_ref[r0:r0 + rb, c0 - di - gn:c0 - di - gn + cn] = y.astype(BF16)
    pad_ref[0:HALO, :] = pad_ref[tm:tm + HALO, :]


def _mamba_in(x, g, w_in, layer, wdt, conv_w, conv_b, dt_bias, di, seq, tm=512):
    t, d = x.shape
    n = conv_w.shape[1]
    gn = (n - di) // 2
    tm = min(tm, seq)
    row = lambda width: pl.BlockSpec((tm, width), lambda i: (i, 0))
    once = pl.Buffered(1)
    full = lambda a: pl.BlockSpec(a.shape, lambda i: (0, 0), pipeline_mode=once)
    taps = jnp.repeat(conv_w, 8, axis=0)
    bias = jnp.broadcast_to(conv_b.reshape(1, n), (8, n))
    consts = [wdt, taps, bias, dt_bias.reshape(1, LANES)]
    return pl.pallas_call(
        functools.partial(_mamba_in_kernel, tiles_per_seq=seq // tm),
        out_shape=[jax.ShapeDtypeStruct((t, di), F32), jax.ShapeDtypeStruct((t, di), F32),
                   jax.ShapeDtypeStruct((t, gn), BF16), jax.ShapeDtypeStruct((t, gn), BF16),
                   jax.ShapeDtypeStruct((t, LANES), F32)],
        grid=(t // tm,),
        in_specs=[row(d), full(g.reshape(1, d)),
                  pl.BlockSpec((None,) + w_in.shape[1:], lambda i: (layer, 0, 0), pipeline_mode=once)]
        + [full(a) for a in consts],
        out_specs=[row(di), row(di), row(gn), row(gn), row(LANES)],
        scratch_shapes=[pltpu.VMEM((tm + HALO, n), F32), pltpu.VMEM((tm, d), BF16)],
        compiler_params=_params("arbitrary"),
        name="mamba_in",
    )(x, g.reshape(1, d), w_in, *consts)


def _ssd_kernel(xs_ref, bm_ref, cm_ref, dt_ref, z_ref, a_ref, d_ref, nw_ref, e2_ref, o_ref, st_ref, ex_ref):
    L = SSD_CHUNK
    n_heads = xs_ref.shape[1] // SSM_HEAD_DIM
    rep = n_heads // SSM_GROUPS
    gw = rep * SSM_HEAD_DIM

    @pl.when(pl.program_id(1) == 0)
    def _():
        st_ref[...] = jnp.zeros_like(st_ref)

    dt = dt_ref[...]
    da = dt * a_ref[...]
    li = lax.broadcasted_iota(jnp.int32, (L, L), 0)
    si = lax.broadcasted_iota(jnp.int32, (L, L), 1)
    causal = li >= si
    tril = causal.astype(BF16)
    hi = da.astype(BF16)
    r1 = da - hi.astype(F32)
    mid = r1.astype(BF16)
    lo = (r1 - mid.astype(F32)).astype(BF16)
    a_cs = _dot(tril, hi) + _dot(tril, mid) + _dot(tril, lo)
    a_cs_t = jnp.transpose(a_cs)
    total = a_cs[L - 1:L, :]
    ea = jnp.exp(a_cs)
    wdec = jnp.exp(total - a_cs) * dt
    cdec = jnp.exp(total)

    left = lax.broadcasted_iota(jnp.int32, (L, LANES), 1) < SSM_HEAD_DIM

    stack = jnp.concatenate([dt, wdec, ea, jnp.broadcast_to(cdec, (16, LANES))], axis=0)
    s_hi, s_lo = _split2(stack)
    ex_ref[...] = _dot(jnp.concatenate([s_hi, s_lo], axis=1), e2_ref[...])

    def group_dots(g):
        bg = bm_ref[:, g * D_STATE:(g + 1) * D_STATE]
        cg = cm_ref[:, g * D_STATE:(g + 1) * D_STATE]
        st = st_ref[g]
        return bg, _dot_nt(cg, bg), st, _dot(cg, st.astype(BF16))

    ahead = group_dots(0)
    for g in range(SSM_GROUPS):
        cols = slice(g * gw, (g + 1) * gw)
        bg, cb, st, y_off = ahead
        if g + 1 < SSM_GROUPS:
            ahead = group_dots(g + 1)
        xg = xs_ref[:, cols]
        xdt_g = xg * ex_ref[0:L, cols]
        ys = []
        for kk in range(rep // 2):
            xdt = xdt_g[:, kk * LANES:(kk + 1) * LANES]
            y_pair = jnp.zeros((L, LANES), F32)
            for side in range(2):
                h = g * rep + 2 * kk + side
                seg = a_cs[:, h:h + 1] - a_cs_t[h:h + 1, :]
                m = (cb * jnp.exp(jnp.where(causal, seg, -jnp.inf))).astype(BF16)
                xh = jnp.where(left if side == 0 else ~left, xdt, 0.0).astype(BF16)
                y_pair = y_pair + _dot(m, xh)
            ys.append(y_pair)
        y_g = jnp.concatenate(ys, axis=1) + y_off * ex_ref[2 * L:3 * L, cols]
        st_ref[g] = ex_ref[3 * L:3 * L + 1, cols] * st + _dot_tn(bg, (xg * ex_ref[L:2 * L, cols]).astype(BF16))
        zg = z_ref[:, g * gw:(g + 1) * gw]
        gy = (y_g + d_ref[:, g * gw:(g + 1) * gw] * xg) * _silu(zg)
        gy = gy * lax.rsqrt(jnp.mean(gy * gy, axis=-1, keepdims=True) + RMS_EPS)
        o_ref[:, g * gw:(g + 1) * gw] = (gy * nw_ref[:, g * gw:(g + 1) * gw]).astype(BF16)


def _ssd(xs, bm, cm, dt, z, a, d_exp, norm_w, bsz, seq):
    t, di = xs.shape
    nc = seq // SSD_CHUNK
    gw = di // SSM_GROUPS
    assert di % (SSM_GROUPS * LANES) == 0 and gw == di // SSM_GROUPS
    blk = lambda width: pl.BlockSpec((SSD_CHUNK, width), lambda b, c: (b * nc + c, 0))
    const = lambda width: pl.BlockSpec((1, width), lambda b, c: (0, 0))
    head_of = np.arange(di) // SSM_HEAD_DIM
    e2 = jnp.asarray(np.tile(np.arange(LANES)[:, None] == head_of[None, :], (2, 1)), dtype=BF16)
    return pl.pallas_call(
        _ssd_kernel,
        out_shape=jax.ShapeDtypeStruct((t, di), BF16),
        grid=(bsz, nc),
        in_specs=[blk(di), blk(bm.shape[1]), blk(cm.shape[1]), blk(LANES), blk(di),
                  const(LANES), const(di), const(di), pl.BlockSpec(e2.shape, lambda b, c: (0, 0))],
        out_specs=blk(di),
        scratch_shapes=[pltpu.VMEM((SSM_GROUPS, D_STATE, gw), F32),
                        pltpu.VMEM((3 * SSD_CHUNK + 16, di), F32)],
        compiler_params=_params("parallel", "arbitrary"),
        name="ssd",
    )(xs, bm, cm, dt, z, a, d_exp, norm_w, e2)


def _attn_layer(x, g, rope, w_in, w_out, sinks, ckp, ckw1, ckw2, cvp, cvw1, cvw2, bsz, seq):
    k, cmp_in, qt, vt, gates_t = _attn_in(x, g, rope, w_in, seq)
    o_a = _swa(qt, k, vt, sinks, bsz, seq)
    half = CMP_STRIDE * HEAD_DIM
    pos = jnp.stack([ckp, cvp]).reshape(2, 2, half)
    w1 = jnp.stack([ckw1, cvw1]).astype(BF16)
    w2 = jnp.stack([ckw2, cvw2]).astype(BF16)
    cmp_n, cmp_t = _compress(cmp_in, pos, w1, w2, bsz, seq)
    o_b = _nsa(qt, k, vt, cmp_n, cmp_t, gates_t, bsz, seq)
    w = w_out.astype(BF16)
    da = o_a.shape[1]
    return _proj_res(x, [o_a, o_b], [w[:da], w[da:]])


def _mamba_layer(x, g, w_in, w_in_f32, layer, conv_w, conv_b, dt_bias, a_log, d_skip, norm_w, w_out, bsz, seq):
    n_heads = dt_bias.shape[0]
    di = n_heads * SSM_HEAD_DIM
    gn = SSM_GROUPS * D_STATE
    padh = LANES - n_heads
    wdt = jnp.pad(w_in_f32[layer][:, 2 * di + 2 * gn:], ((0, 0), (0, padh))).astype(BF16)
    z, xs, bm, cm, dt = _mamba_in(x, g, w_in, layer, wdt, conv_w, conv_b, jnp.pad(dt_bias, (0, padh)), di, seq)
    a = jnp.pad(-jnp.exp(a_log), (0, padh)).reshape(1, LANES)
    d_exp = jnp.repeat(d_skip, SSM_HEAD_DIM).reshape(1, di)
    yn = _ssd(xs, bm, cm, dt, z, a, d_exp, norm_w.reshape(1, di), bsz, seq)
    return _proj_res(x, [yn], [w_out.astype(BF16)])


def kernel(x, norm_gains, final_norm, ffn_w_gate, ffn_w_up, ffn_w_down, attn_w_in, attn_w_out, attn_sinks, cmp_k_pos, cmp_k_w1, cmp_k_w2, cmp_v_pos, cmp_v_w1, cmp_v_w2, ssm_w_in, ssm_conv_w, ssm_conv_b, ssm_dt_bias, ssm_a_log, ssm_d, ssm_norm, ssm_w_out):
    bsz, seq, d = x.shape
    depth = norm_gains.shape[0]
    rope = _rope_tables(seq)
    wg, wu, wd = ffn_w_gate, ffn_w_up, ffn_w_down
    w_ssm = ssm_w_in.astype(BF16)
    x = x.reshape(bsz * seq, d)
    for i in range(depth):
        g = norm_gains[i]
        x = _ffn(x, g[0], wg, wu, wd, i, 0)
        j = i // 2
        if i % 2 == 0:
            x = _attn_layer(x, g[1], rope, attn_w_in[j], attn_w_out[j], attn_sinks[j],
                            cmp_k_pos[j], cmp_k_w1[j], cmp_k_w2[j],
                            cmp_v_pos[j], cmp_v_w1[j], cmp_v_w2[j], bsz, seq)
        else:
            x = _mamba_layer(x, g[1], w_ssm, ssm_w_in, j, ssm_conv_w[j], ssm_conv_b[j], ssm_dt_bias[j],
                             ssm_a_log[j], ssm_d[j], ssm_norm[j], ssm_w_out[j], bsz, seq)
        x = _ffn(x, g[2], wg, wu, wd, i, 1, final_gain=final_norm if i == depth - 1 else None)
    return x.reshape(bsz, seq, d)
```

```python
import functools
import math

import numpy as np
import jax
import jax.numpy as jnp
from jax import lax
from jax.experimental import pallas as pl
from jax.experimental.pallas import tpu as pltpu

HEAD_DIM = 64
ROPE_THETA = 10000.0
Q_BLOCK = 128
SWA_Q_HEADS = 8
SWA_KV_HEADS = 2
SWA_WINDOW = 128
NSA_Q_HEADS = 8
NSA_KV_HEADS = 2
CMP_LEN = 32
CMP_STRIDE = 16
CMP_HIDDEN = 128
SEL_LEN = 64
SEL_TOPK = 16
NSA_WINDOW = 512
N_BRANCH = 3
SSM_HEAD_DIM = 64
SSM_GROUPS = 8
D_STATE = 128
CONV_W = 4
SSD_CHUNK = 128
RMS_EPS = 1e-6
NEG_INF = -1e30
SEL_FORCE = 1e4
LOG2E = math.log2(math.e)

LANES = 128
VMEM_LIMIT = 56 * 1024 * 1024
BF16 = jnp.bfloat16
F32 = jnp.float32


def _params(*sem, flags=None):
    return pltpu.CompilerParams(dimension_semantics=sem, vmem_limit_bytes=VMEM_LIMIT, flags=flags)


def _dot(a, b):
    return jnp.dot(a, b, preferred_element_type=F32)


def _dot_nt(a, b):
    return lax.dot_general(a, b, (((1,), (1,)), ((), ())), preferred_element_type=F32)


def _dot_tn(a, b):
    return lax.dot_general(a, b, (((0,), (0,)), ((), ())), preferred_element_type=F32)


def _split2(v):
    hi = v.astype(BF16)
    lo = (v - hi.astype(F32)).astype(BF16)
    return hi, lo


def _silu(x):
    hx = 0.5 * x
    return hx + hx * jnp.tanh(hx)


def _sigmoid(x):
    return 0.5 + 0.5 * jnp.tanh(0.5 * x)


def _rms(x, g):
    return x * lax.rsqrt(jnp.mean(x * x, axis=-1, keepdims=True) + RMS_EPS) * g


def _ffn_kernel(x_ref, g_ref, wg_ref, wu_ref, wd_ref, fg_ref, o_ref, *, final_norm, tf):
    x = x_ref[...]
    h = _rms(x, g_ref[...]).astype(BF16)
    acc = None
    for f0 in range(0, wg_ref.shape[1], tf):
        gate = _dot(h, wg_ref[:, f0:f0 + tf].astype(BF16))
        up = _dot(h, wu_ref[:, f0:f0 + tf].astype(BF16))
        part = _dot((_silu(gate) * up).astype(BF16), wd_ref[f0:f0 + tf, :].astype(BF16))
        acc = part if acc is None else acc + part
    y = x + 0.5 * acc
    if final_norm:
        y = _rms(y, fg_ref[...])
    o_ref[...] = y


def _ffn(x, g, wg, wu, wd, layer, slot, final_gain=None, tm=512, tf=256):
    t, d = x.shape
    f = wg.shape[-1]
    tm = min(tm, t)
    fg = jnp.ones((1, d), F32) if final_gain is None else final_gain.reshape(1, d)
    once = pl.Buffered(1)
    return pl.pallas_call(
        functools.partial(_ffn_kernel, final_norm=final_gain is not None, tf=tf),
        out_shape=jax.ShapeDtypeStruct((t, d), F32),
        grid=(t // tm,),
        in_specs=[
            pl.BlockSpec((tm, d), lambda i: (i, 0)),
            pl.BlockSpec((1, d), lambda i: (0, 0)),
            pl.BlockSpec((None, None, d, f), lambda i: (layer, slot, 0, 0), pipeline_mode=once),
            pl.BlockSpec((None, None, d, f), lambda i: (layer, slot, 0, 0), pipeline_mode=once),
            pl.BlockSpec((None, None, f, d), lambda i: (layer, slot, 0, 0), pipeline_mode=once),
            pl.BlockSpec((1, d), lambda i: (0, 0)),
        ],
        out_specs=pl.BlockSpec((tm, d), lambda i: (i, 0)),
        compiler_params=_params("parallel"),
        name="ffn",
    )(x, g.reshape(1, d), wg, wu, wd, fg)


def _proj_res_kernel(*refs, n_in):
    x_ref = refs[0]
    o_ref = refs[1 + 2 * n_in]
    y = x_ref[...]
    for i in range(n_in):
        y = y + _dot(refs[1 + i][...], refs[1 + n_in + i][...])
    o_ref[...] = y


def _proj_res(x, acts, ws, tm=1024):
    t, d = x.shape
    tm = min(tm, t)
    n_in = len(acts)
    in_specs = [pl.BlockSpec((tm, d), lambda i: (i, 0))]
    in_specs += [pl.BlockSpec((tm, a.shape[1]), lambda i: (i, 0)) for a in acts]
    in_specs += [pl.BlockSpec(w.shape, lambda i: (0, 0)) for w in ws]
    return pl.pallas_call(
        functools.partial(_proj_res_kernel, n_in=n_in),
        out_shape=jax.ShapeDtypeStruct((t, d), F32),
        grid=(t // tm,),
        in_specs=in_specs,
        out_specs=pl.BlockSpec((tm, d), lambda i: (i, 0)),
        compiler_params=_params("parallel"),
        name="proj_res",
    )(x, *acts, *ws)


def _rope_lanes(y, c, s_lo, s_hi):
    return y * c + pltpu.roll(y, 96, 1) * s_lo + pltpu.roll(y, 32, 1) * s_hi


GATE_ROWS = 32
HALF = HEAD_DIM // 2


def _attn_in_kernel(x_ref, g_ref, c_ref, slo_ref, shi_ref, ct_ref, st_ref,
                    wk_ref, wcmp_ref, wqt_ref, wvt_ref, wgt_ref,
                    k_ref, cmp_ref, qt_ref, vt_ref, gt_ref):
    h = _rms(x_ref[...], g_ref[...]).astype(BF16)
    c, s_lo, s_hi = c_ref[...], slo_ref[...], shi_ref[...]
    yk = _dot(h, wk_ref[...])
    k_ref[...] = jnp.concatenate(
        [_rope_lanes(yk[:, i * LANES:(i + 1) * LANES], c, s_lo, s_hi) for i in range(3)], axis=1).astype(BF16)
    yc = _dot(h, wcmp_ref[...])
    cmp_ref[...] = jnp.concatenate([_rope_lanes(yc[:, :LANES], c, s_lo, s_hi), yc[:, LANES:]], axis=1)

    ct, st = ct_ref[...], st_ref[...]
    scale = HEAD_DIM ** -0.5 * LOG2E
    yq = _dot_nt(wqt_ref[...], h)
    for hd in range(yq.shape[0] // HEAD_DIM):
        lo = hd * HEAD_DIM
        y1, y2 = yq[lo:lo + HALF], yq[lo + HALF:lo + HEAD_DIM]
        qt_ref[lo:lo + HALF, :] = ((y1 * ct - y2 * st) * scale).astype(BF16)
        qt_ref[lo + HALF:lo + HEAD_DIM, :] = ((y2 * ct + y1 * st) * scale).astype(BF16)
    yv = _dot_nt(wvt_ref[...], h).astype(BF16)
    for cidx in range(yv.shape[1] // Q_BLOCK):
        vt_ref[cidx] = yv[:, cidx * Q_BLOCK:(cidx + 1) * Q_BLOCK]
    gt_ref[...] = _sigmoid(_dot_nt(wgt_ref[...], h))


def _rope_tables(seq):
    inv = 1.0 / (ROPE_THETA ** (jnp.arange(0, HEAD_DIM, 2, dtype=F32) / HEAD_DIM))
    ang = jnp.arange(seq, dtype=F32)[:, None] * inv[None, :]
    cos, sin = jnp.cos(ang), jnp.sin(ang)
    zero = jnp.zeros_like(sin)
    c = jnp.tile(cos, (1, 4))
    s_lo = jnp.tile(jnp.concatenate([-sin, zero], axis=1), (1, 2))
    s_hi = jnp.tile(jnp.concatenate([zero, sin], axis=1), (1, 2))
    return c, s_lo, s_hi, cos.T, sin.T


def _attn_in(x, g, rope, w_in, seq, tm=512):
    t, d = x.shape
    tm = min(tm, seq)
    dq = SWA_Q_HEADS * HEAD_DIM
    dkv = SWA_KV_HEADS * HEAD_DIM
    w = w_in.astype(BF16)
    cuts = np.cumsum([dq, dkv, dkv, dq] + [dkv] * 6)
    wqa, wka, wva, wqb, wkc, wvc, wks, wvs, wkw, wvw, wgt = jnp.split(w, cuts, axis=1)
    wk = jnp.concatenate([wka, wks, wkw], axis=1)
    wcmp = jnp.concatenate([wkc, wvc], axis=1)
    wqt = jnp.concatenate([wqa, wqb], axis=1).T
    wvt = jnp.concatenate([wva, wvs, wvw], axis=1).T
    wgt = jnp.pad(wgt, ((0, 0), (0, GATE_ROWS - wgt.shape[1]))).T
    ws = [wk, wcmp, wqt, wvt, wgt]
    n_pos = seq // tm
    nvt = tm // Q_BLOCK
    row = lambda i: (i, 0)
    col = lambda i: (0, i)
    return pl.pallas_call(
        _attn_in_kernel,
        out_shape=[
            jax.ShapeDtypeStruct((t, wk.shape[1]), BF16),
            jax.ShapeDtypeStruct((t, wcmp.shape[1]), F32),
            jax.ShapeDtypeStruct((wqt.shape[0], t), BF16),
            jax.ShapeDtypeStruct((t // Q_BLOCK, wvt.shape[0], Q_BLOCK), BF16),
            jax.ShapeDtypeStruct((GATE_ROWS, t), F32),
        ],
        grid=(t // tm,),
        in_specs=[pl.BlockSpec((tm, d), row), pl.BlockSpec((1, d), lambda i: (0, 0))]
        + [pl.BlockSpec((tm, LANES), lambda i: (i % n_pos, 0))] * 3
        + [pl.BlockSpec((HALF, tm), lambda i: (0, i % n_pos))] * 2
        + [pl.BlockSpec(wi.shape, lambda i: (0, 0)) for wi in ws],
        out_specs=[
            pl.BlockSpec((tm, wk.shape[1]), row),
            pl.BlockSpec((tm, wcmp.shape[1]), row),
            pl.BlockSpec((wqt.shape[0], tm), col),
            pl.BlockSpec((nvt, wvt.shape[0], Q_BLOCK), lambda i: (i, 0, 0)),
            pl.BlockSpec((GATE_ROWS, tm), col),
        ],
        compiler_params=_params("parallel"),
        name="attn_in",
    )(x, g.reshape(1, d), *rope, *ws)


def _group_q(qt_ref, g, rep):
    qs = jnp.concatenate(
        [qt_ref[(g * rep + r) * HEAD_DIM:(g * rep + r + 1) * HEAD_DIM, :] for r in range(rep)], axis=1)
    z = jnp.zeros_like(qs)
    return qs, jnp.concatenate([qs, z] if g == 0 else [z, qs], axis=0)


def _store_heads(o_ref, ot, g, rep):
    for p in range(rep // 2):
        pair = jnp.concatenate([ot[:, (2 * p) * Q_BLOCK:(2 * p + 1) * Q_BLOCK],
                                ot[:, (2 * p + 1) * Q_BLOCK:(2 * p + 2) * Q_BLOCK]], axis=0)
        c0 = (g * (rep // 2) + p) * LANES
        o_ref[:, c0:c0 + LANES] = jnp.transpose(pair).astype(BF16)


SUM_ROWS = 16


def _vt_tiles(vt_ref, tile0, ntiles, kind, g):
    lo = (kind * 2 + g) * HEAD_DIM
    vt = jnp.concatenate([vt_ref[tile0 + c, lo:lo + HEAD_DIM, :] for c in range(ntiles)], axis=1)
    return jnp.concatenate([vt, jnp.ones((SUM_ROWS, vt.shape[1]), BF16)], axis=0)


def _swa_kernel(sink_ref, qt_ref, k_ref, vt_ref, o_ref):
    n = pl.program_id(1)
    rep = SWA_Q_HEADS // SWA_KV_HEADS
    nq = rep * Q_BLOCK
    nkeys = 2 * Q_BLOCK
    tile0 = jnp.maximum(n - 1, 0)
    start = pl.multiple_of(tile0 * Q_BLOCK, Q_BLOCK)
    lane = lax.broadcasted_iota(jnp.int32, (nkeys, nq), 1)
    diff = n * Q_BLOCK + (lane & (Q_BLOCK - 1)) - start - lax.broadcasted_iota(jnp.int32, (nkeys, nq), 0)
    mask = diff.astype(jnp.uint32) < SWA_WINDOW
    head = lax.broadcasted_iota(jnp.int32, (1, nq), 1) // Q_BLOCK
    k2 = k_ref[pl.ds(start, nkeys), 0:LANES]
    scores = [_dot(k2, _group_q(qt_ref, g, rep)[1]) for g in range(SWA_KV_HEADS)]
    for g in range(SWA_KV_HEADS):
        s = jnp.where(mask, scores[g], NEG_INF)
        sink = jnp.zeros((1, nq), F32)
        for r in range(rep):
            sink = jnp.where(head == r, sink_ref[g * rep + r] * LOG2E, sink)
        m = jnp.maximum(jnp.max(s, axis=0, keepdims=True), sink)
        e = jnp.exp2(s - jnp.where(m > NEG_INF, m, 0.0))
        ot = _dot(_vt_tiles(vt_ref, tile0, 2, 0, g), e.astype(BF16))
        den = ot[HEAD_DIM:HEAD_DIM + 1] + jnp.exp2(sink - m)
        _store_heads(o_ref, ot[0:HEAD_DIM] / jnp.maximum(den, 1e-30), g, rep)


def _swa(qt, k, vt, sinks, bsz, seq):
    nq = seq // Q_BLOCK
    dq = SWA_Q_HEADS * HEAD_DIM
    return pl.pallas_call(
        _swa_kernel,
        out_shape=jax.ShapeDtypeStruct((bsz * seq, dq), BF16),
        grid=(bsz, nq),
        in_specs=[
            pl.BlockSpec(memory_space=pltpu.SMEM),
            pl.BlockSpec((dq, Q_BLOCK), lambda b, n: (0, b * nq + n)),
            pl.BlockSpec((seq, k.shape[1]), lambda b, n: (b, 0)),
            pl.BlockSpec((nq,) + vt.shape[1:], lambda b, n: (b, 0, 0)),
        ],
        out_specs=pl.BlockSpec((Q_BLOCK, dq), lambda b, n: (b * nq + n, 0)),
        compiler_params=_params("parallel", "arbitrary"),
        name="swa",
    )(sinks, qt, k, vt)


def _compress_kernel(r_ref, pos_ref, w1_ref, w2_ref, w2t_ref, o_ref, ot_ref):
    half = CMP_STRIDE * HEAD_DIM
    r = r_ref[...]
    top = _dot((r + pos_ref[0:1, :]).astype(BF16), w1_ref[0:half, :])
    bot = _dot((r + pos_ref[1:2, :]).astype(BF16), w1_ref[half:2 * half, :])
    pre = top + pltpu.roll(bot, bot.shape[0] - 1, 0)
    hid = jax.nn.gelu(pre).astype(BF16)
    o_ref[...] = _dot(hid, w2_ref[...]).astype(BF16)
    ot_ref[...] = _dot_nt(w2t_ref[...], hid).astype(BF16)


def _compress(cmp_in, pos, w1, w2, bsz, seq):
    ng = NSA_KV_HEADS
    nr = seq // CMP_STRIDE
    half = CMP_STRIDE * HEAD_DIM
    rows = cmp_in.reshape(bsz, nr, CMP_STRIDE, 2 * ng, HEAD_DIM).transpose(0, 3, 1, 2, 4)
    rows = rows.reshape(bsz, 2 * ng, nr, half)
    return pl.pallas_call(
        _compress_kernel,
        out_shape=[jax.ShapeDtypeStruct((bsz, 2 * ng, nr, HEAD_DIM), BF16),
                   jax.ShapeDtypeStruct((bsz, 2 * ng, HEAD_DIM, nr), BF16)],
        grid=(bsz, 2 * ng),
        in_specs=[
            pl.BlockSpec((None, None, nr, half), lambda b, j: (b, j, 0, 0)),
            pl.BlockSpec((None, 2, half), lambda b, j: (j // ng, 0, 0)),
            pl.BlockSpec((None, 2 * half, CMP_HIDDEN), lambda b, j: (j // ng, 0, 0)),
            pl.BlockSpec((None, CMP_HIDDEN, HEAD_DIM), lambda b, j: (j // ng, 0, 0)),
            pl.BlockSpec((None, HEAD_DIM, CMP_HIDDEN), lambda b, j: (j // ng, 0, 0)),
        ],
        out_specs=[pl.BlockSpec((None, None, nr, HEAD_DIM), lambda b, j: (b, j, 0, 0)),
                   pl.BlockSpec((None, None, HEAD_DIM, nr), lambda b, j: (b, j, 0, 0))],
        compiler_params=_params("parallel", "parallel"),
        name="compress",
    )(rows, pos, w1, w2, jnp.swapaxes(w2, 1, 2))


def _softmax_weights(s, mask):
    s = jnp.where(mask, s, NEG_INF)
    m = jnp.max(s, axis=0, keepdims=True)
    return jnp.exp2(s - jnp.where(m > NEG_INF, m, 0.0))


def _stable_topk_mask(score, k):
    nb = score.shape[0]
    jj = lax.broadcasted_iota(jnp.int32, (8, score.shape[1]), 0)
    rank = jnp.zeros(score.shape, F32)
    for j in range(nb):
        rj = score[j:j + 1, :]
        v0 = (j // 8) * 8
        mid = score[v0:v0 + 8]
        parts = [jnp.where((rj > mid) | ((rj == mid) & (jj + v0 > j)), 1.0, 0.0)]
        if v0 > 0:
            parts.insert(0, jnp.where(rj > score[0:v0], 1.0, 0.0))
        if v0 + 8 < nb:
            parts.append(jnp.where(rj >= score[v0 + 8:nb], 1.0, 0.0))
        rank = rank + (jnp.concatenate(parts, axis=0) if len(parts) > 1 else parts[0])
    return jnp.where(rank < k, 1.0, 0.0)


def _nsa_kernel(qt_ref, k_ref, vt_ref, cn_ref, ct_ref, gt_ref, wcs_ref, o_ref, sel_ref, part_ref,
                m_ref, acc_ref, *, seq, kt):
    n = pl.program_id(1)
    rep = NSA_Q_HEADS // NSA_KV_HEADS
    ng = NSA_KV_HEADS
    nq = rep * Q_BLOCK
    n_cmp = seq // CMP_STRIDE
    n_blk = seq // SEL_LEN
    bpt = kt // SEL_LEN
    tpk = kt // Q_BLOCK
    t0 = n * Q_BLOCK
    tq = t0 + (lax.broadcasted_iota(jnp.int32, (1, nq), 1) & (Q_BLOCK - 1))

    def gate(g, c):
        return jnp.concatenate(
            [gt_ref[(g * rep + r) * N_BRANCH + c:(g * rep + r) * N_BRANCH + c + 1, :] for r in range(rep)], axis=1)

    qs2s = []
    cmp_mask = lax.broadcasted_iota(jnp.int32, (n_cmp, nq), 0) * CMP_STRIDE + (CMP_LEN - 1) <= tq
    nwin = NSA_WINDOW + Q_BLOCK
    wt0 = jnp.maximum(n + 1 - nwin // Q_BLOCK, 0)
    start = pl.multiple_of(wt0 * Q_BLOCK, Q_BLOCK)
    diff = tq - start - lax.broadcasted_iota(jnp.int32, (nwin, nq), 0)
    win_mask = diff.astype(jnp.uint32) < NSA_WINDOW
    jb = lax.broadcasted_iota(jnp.int32, (n_blk, Q_BLOCK), 0)
    cur = (t0 + lax.broadcasted_iota(jnp.int32, (n_blk, Q_BLOCK), 1)) // SEL_LEN
    valid = jb <= cur
    forced = valid & ((jb == 0) | (jb == cur) | (jb == cur - 1))
    kwin = k_ref[pl.ds(start, nwin), 2 * LANES:3 * LANES]
    s_cmp, s_win = [], []
    for g in range(ng):
        qs, qs2 = _group_q(qt_ref, g, rep)
        qs2s.append(qs2)
        s_cmp.append(_dot(cn_ref[g], qs))
        s_win.append(_dot(kwin, qs2))
    for g in range(ng):
        e_cmp = _softmax_weights(s_cmp[g], cmp_mask)
        p_cmp = e_cmp / jnp.maximum(jnp.sum(e_cmp, axis=0, keepdims=True), 1e-30)
        o_cmp = _dot(ct_ref[ng + g], p_cmp.astype(BF16))
        psum = p_cmp[:, 0:Q_BLOCK]
        for r in range(1, rep):
            psum = psum + p_cmp[:, r * Q_BLOCK:(r + 1) * Q_BLOCK]
        p_hi, p_lo = _split2(psum)
        imp = _dot(wcs_ref[...], p_hi) + _dot(wcs_ref[...], p_lo)
        score = imp + jnp.where(forced, SEL_FORCE, 0.0) - jnp.where(valid, 0.0, SEL_FORCE)
        sel = _stable_topk_mask(score, min(SEL_TOPK, n_blk))
        sel_ref[g] = jnp.concatenate([sel] * rep, axis=1)

        e_win = _softmax_weights(s_win[g], win_mask)
        o_win = _dot(_vt_tiles(vt_ref, wt0, nwin // Q_BLOCK, 2, g), e_win.astype(BF16))
        o_win = o_win[0:HEAD_DIM] / jnp.maximum(o_win[HEAD_DIM:HEAD_DIM + 1], 1e-30)
        part_ref[g] = gate(g, 0) * o_cmp + gate(g, 2) * o_win

    def scores(i):
        kblk = k_ref[pl.ds(pl.multiple_of(i * kt, kt), kt), LANES:2 * LANES]
        return tuple(_dot(kblk, qs2s[g]) for g in range(ng))

    def tile(i, s_all, carry, diagonal):
        if diagonal:
            causal = i * kt + lax.broadcasted_iota(jnp.int32, (kt, nq), 0) <= tq
        hw = nq // 2
        for g in range(ng):
            vt = _vt_tiles(vt_ref, i * tpk, tpk, 1, g)
            selb_all = sel_ref[g, pl.ds(pl.multiple_of(i * bpt, bpt), bpt), :]
            for ls in (slice(0, hw), slice(hw, nq)):
                m = m_ref[g, :, ls]
                s = s_all[g][:, ls]
                selb = selb_all[:, ls]
                s = jnp.concatenate(
                    [jnp.where(selb[b:b + 1, :] > 0.5, s[b * SEL_LEN:(b + 1) * SEL_LEN, :], NEG_INF)
                     for b in range(bpt)], axis=0)
                if diagonal:
                    s = jnp.where(causal[:, ls], s, NEG_INF)
                m_new = jnp.maximum(m, jnp.max(s, axis=0, keepdims=True))
                e = jnp.exp2(s - jnp.where(m_new > NEG_INF, m_new, 0.0))
                acc_ref[g, :, ls] = jnp.exp2(m - m_new) * acc_ref[g, :, ls] + _dot(vt, e.astype(BF16))
                m_ref[g, :, ls] = m_new
        return carry

    m_ref[...] = jnp.full(m_ref.shape, NEG_INF, F32)
    acc_ref[...] = jnp.zeros(acc_ref.shape, F32)
    n_full = t0 // kt
    lax.fori_loop(0, n_full, lambda i, c: tile(i, scores(i), c, False), 0)
    tile(n_full, scores(n_full), 0, True)
    for g in range(ng):
        acc = acc_ref[g]
        o_slc = acc[0:HEAD_DIM] / jnp.maximum(acc[HEAD_DIM:HEAD_DIM + 1], 1e-30)
        _store_heads(o_ref, part_ref[g] + gate(g, 1) * o_slc, g, rep)


def _cmp_to_sel_weights_t(seq):
    n_cmp = (seq - CMP_LEN) // CMP_STRIDE + 1
    n_blk = seq // SEL_LEN
    cs = np.arange(n_cmp) * CMP_STRIDE
    ss = np.arange(n_blk) * SEL_LEN
    ov = np.minimum(cs[:, None] + CMP_LEN, ss[None, :] + SEL_LEN) - np.maximum(cs[:, None], ss[None, :])
    w = (np.clip(ov, 0, None) / CMP_LEN).astype(np.float32)
    w = np.concatenate([w, np.zeros((seq // CMP_STRIDE - n_cmp, n_blk), np.float32)], axis=0)
    return jnp.asarray(w.T, dtype=BF16)


def _nsa(qt, k, vt, cmp_n, cmp_t, gates_t, bsz, seq, kt=512):
    nq = seq // Q_BLOCK
    dq = NSA_Q_HEADS * HEAD_DIM
    kt = min(kt, seq)
    n_blk = seq // SEL_LEN
    assert NSA_WINDOW + Q_BLOCK <= seq and (kt // SEL_LEN) % 8 == 0
    return pl.pallas_call(
        functools.partial(_nsa_kernel, seq=seq, kt=kt),
        out_shape=jax.ShapeDtypeStruct((bsz * seq, dq), BF16),
        grid=(bsz, nq),
        in_specs=[
            pl.BlockSpec((dq, Q_BLOCK), lambda b, n: (1, b * nq + n)),
            pl.BlockSpec((seq, k.shape[1]), lambda b, n: (b, 0)),
            pl.BlockSpec((nq,) + vt.shape[1:], lambda b, n: (b, 0, 0)),
            pl.BlockSpec((None,) + cmp_n.shape[1:], lambda b, n: (b, 0, 0, 0)),
            pl.BlockSpec((None,) + cmp_t.shape[1:], lambda b, n: (b, 0, 0, 0)),
            pl.BlockSpec((GATE_ROWS, Q_BLOCK), lambda b, n: (0, b * nq + n)),
            pl.BlockSpec((n_blk, seq // CMP_STRIDE), lambda b, n: (0, 0)),
        ],
        out_specs=pl.BlockSpec((Q_BLOCK, dq), lambda b, n: (b * nq + n, 0)),
        scratch_shapes=[pltpu.VMEM((NSA_KV_HEADS, n_blk, (NSA_Q_HEADS // NSA_KV_HEADS) * Q_BLOCK), F32),
                        pltpu.VMEM((NSA_KV_HEADS, HEAD_DIM, (NSA_Q_HEADS // NSA_KV_HEADS) * Q_BLOCK), F32),
                        pltpu.VMEM((NSA_KV_HEADS, 1, (NSA_Q_HEADS // NSA_KV_HEADS) * Q_BLOCK), F32),
                        pltpu.VMEM((NSA_KV_HEADS, HEAD_DIM + SUM_ROWS,
                                    (NSA_Q_HEADS // NSA_KV_HEADS) * Q_BLOCK), F32)],
        compiler_params=_params("parallel", "arbitrary"),
        name="nsa",
    )(qt, k, vt, cmp_n, cmp_t, gates_t, _cmp_to_sel_weights_t(seq))


HALO = 8
CONV_COLS = 512
CONV_ROWS = 32


def _mamba_in_kernel(x_ref, g_ref, w_ref, wdt_ref, cw_ref, cb_ref, dtb_ref,
                     z_ref, xs_ref, bm_ref, cm_ref, dt_ref, pad_ref, h_ref, *, tiles_per_seq):
    tm = x_ref.shape[0]
    di = xs_ref.shape[1]
    gn = bm_ref.shape[1]
    n = pad_ref.shape[1]

    @pl.when(pl.program_id(0) % tiles_per_seq == 0)
    def _():
        pad_ref[0:HALO, :] = jnp.zeros((HALO, n), F32)

    h_ref[...] = _rms(x_ref[...], g_ref[...]).astype(BF16)
    dt_ref[...] = jax.nn.softplus(_dot(h_ref[...], wdt_ref[...]) + dtb_ref[...])
    zc = di * CONV_COLS // n
    cn, rb = CONV_COLS, CONV_ROWS

    def project(c0):
        pad_ref[HALO:HALO + tm, c0:c0 + cn] = _dot(h_ref[...], w_ref[:, di + c0:di + c0 + cn])
        z0 = (c0 // cn) * zc
        z_ref[:, z0:z0 + zc] = _dot(h_ref[...], w_ref[:, z0:z0 + zc])

    project(0)
    for c0 in range(0, n, cn):
        if c0 + cn < n:
            project(c0 + cn)
        taps = [cw_ref[8 * k:8 * k + 8, c0:c0 + cn][None] for k in range(CONV_W)]
        bias = cb_ref[:, c0:c0 + cn][None]
        for r0 in range(0, tm, rb):
            y = bias
            for k in range(CONV_W):
                y = y + taps[CONV_W - 1 - k] * pad_ref[HALO + r0 - k:HALO + r0 - k + rb,
                                                       c0:c0 + cn].reshape(rb // 8, 8, cn)
            y = _silu(y).reshape(rb, cn)
            if c0 < di:
                xs_ref[r0:r0 + rb, c0:c0 + cn] = y
            elif c0 < di + gn:
                bm_ref[r0:r0 + rb, c0 - di:c0 - di + cn] = y.astype(BF16)
            else:
                cm_ref[r0:r0 + rb, c0 - di - gn:c0 - di - gn + cn] = y.astype(BF16)
    pad_ref[0:HALO, :] = pad_ref[tm:tm + HALO, :]


def _mamba_in(x, g, w_in, layer, wdt, conv_w, conv_b, dt_bias, di, seq, tm=512):
    t, d = x.shape
    n = conv_w.shape[1]
    gn = (n - di) // 2
    tm = min(tm, seq)
    row = lambda width: pl.BlockSpec((tm, width), lambda i: (i, 0))
    once = pl.Buffered(1)
    full = lambda a: pl.BlockSpec(a.shape, lambda i: (0, 0), pipeline_mode=once)
    taps = jnp.repeat(conv_w, 8, axis=0)
    bias = jnp.broadcast_to(conv_b.reshape(1, n), (8, n))
    consts = [wdt, taps, bias, dt_bias.reshape(1, LANES)]
    return pl.pallas_call(
        functools.partial(_mamba_in_kernel, tiles_per_seq=seq // tm),
        out_shape=[jax.ShapeDtypeStruct((t, di), F32), jax.ShapeDtypeStruct((t, di), F32),
                   jax.ShapeDtypeStruct((t, gn), BF16), jax.ShapeDtypeStruct((t, gn), BF16),
                   jax.ShapeDtypeStruct((t, LANES), F32)],
        grid=(t // tm,),
        in_specs=[row(d), full(g.reshape(1, d)),
                  pl.BlockSpec((None,) + w_in.shape[1:], lambda i: (layer, 0, 0), pipeline_mode=once)]
        + [full(a) for a in consts],
        out_specs=[row(di), row(di), row(gn), row(gn), row(LANES)],
        scratch_shapes=[pltpu.VMEM((tm + HALO, n), F32), pltpu.VMEM((tm, d), BF16)],
        compiler_params=_params("arbitrary"),
        name="mamba_in",
    )(x, g.reshape(1, d), w_in, *consts)


def _ssd_kernel(xs_ref, bm_ref, cm_ref, dt_ref, z_ref, a_ref, d_ref, nw_ref, e2_ref, o_ref, st_ref, ex_ref):
    L = SSD_CHUNK
    n_heads = xs_ref.shape[1] // SSM_HEAD_DIM
    rep = n_heads // SSM_GROUPS
    gw = rep * SSM_HEAD_DIM

    @pl.when(pl.program_id(1) == 0)
    def _():
        st_ref[...] = jnp.zeros_like(st_ref)

    dt = dt_ref[...]
    da = dt * a_ref[...]
    li = lax.broadcasted_iota(jnp.int32, (L, L), 0)
    si = lax.broadcasted_iota(jnp.int32, (L, L), 1)
    causal = li >= si
    tril = causal.astype(BF16)
    hi = da.astype(BF16)
    r1 = da - hi.astype(F32)
    mid = r1.astype(BF16)
    lo = (r1 - mid.astype(F32)).astype(BF16)
    a_cs = _dot(tril, hi) + _dot(tril, mid) + _dot(tril, lo)
    a_cs_t = jnp.transpose(a_cs)
    total = a_cs[L - 1:L, :]
    ea = jnp.exp(a_cs)
    wdec = jnp.exp(total - a_cs) * dt
    cdec = jnp.exp(total)

    left = lax.broadcasted_iota(jnp.int32, (L, LANES), 1) < SSM_HEAD_DIM

    stack = jnp.concatenate([dt, wdec, ea, jnp.broadcast_to(cdec, (16, LANES))], axis=0)
    s_hi, s_lo = _split2(stack)
    ex_ref[...] = _dot(jnp.concatenate([s_hi, s_lo], axis=1), e2_ref[...])

    def group_dots(g):
        bg = bm_ref[:, g * D_STATE:(g + 1) * D_STATE]
        cg = cm_ref[:, g * D_STATE:(g + 1) * D_STATE]
        st = st_ref[g]
        return bg, _dot_nt(cg, bg), st, _dot(cg, st.astype(BF16))

    ahead = group_dots(0)
    for g in range(SSM_GROUPS):
        cols = slice(g * gw, (g + 1) * gw)
        bg, cb, st, y_off = ahead
        if g + 1 < SSM_GROUPS:
            ahead = group_dots(g + 1)
        xg = xs_ref[:, cols]
        xdt_g = xg * ex_ref[0:L, cols]
        ys = []
        for kk in range(rep // 2):
            xdt = xdt_g[:, kk * LANES:(kk + 1) * LANES]
            y_pair = jnp.zeros((L, LANES), F32)
            for side in range(2):
                h = g * rep + 2 * kk + side
                seg = a_cs[:, h:h + 1] - a_cs_t[h:h + 1, :]
                m = (cb * jnp.exp(jnp.where(causal, seg, -jnp.inf))).astype(BF16)
                xh = jnp.where(left if side == 0 else ~left, xdt, 0.0).astype(BF16)
                y_pair = y_pair + _dot(m, xh)
            ys.append(y_pair)
        y_g = jnp.concatenate(ys, axis=1) + y_off * ex_ref[2 * L:3 * L, cols]
        st_ref[g] = ex_ref[3 * L:3 * L + 1, cols] * st + _dot_tn(bg, (xg * ex_ref[L:2 * L, cols]).astype(BF16))
        zg = z_ref[:, g * gw:(g + 1) * gw]
        gy = (y_g + d_ref[:, g * gw:(g + 1) * gw] * xg) * _silu(zg)
        gy = gy * lax.rsqrt(jnp.mean(gy * gy, axis=-1, keepdims=True) + RMS_EPS)
        o_ref[:, g * gw:(g + 1) * gw] = (gy * nw_ref[:, g * gw:(g + 1) * gw]).astype(BF16)


def _ssd(xs, bm, cm, dt, z, a, d_exp, norm_w, bsz, seq):
    t, di = xs.shape
    nc = seq // SSD_CHUNK
    gw = di // SSM_GROUPS
    assert di % (SSM_GROUPS * LANES) == 0 and gw == di // SSM_GROUPS
    blk = lambda width: pl.BlockSpec((SSD_CHUNK, width), lambda b, c: (b * nc + c, 0))
    const = lambda width: pl.BlockSpec((1, width), lambda b, c: (0, 0))
    head_of = np.arange(di) // SSM_HEAD_DIM
    e2 = jnp.asarray(np.tile(np.arange(LANES)[:, None] == head_of[None, :], (2, 1)), dtype=BF16)
    return pl.pallas_call(
        _ssd_kernel,
        out_shape=jax.ShapeDtypeStruct((t, di), BF16),
        grid=(bsz, nc),
        in_specs=[blk(di), blk(bm.shape[1]), blk(cm.shape[1]), blk(LANES), blk(di),
                  const(LANES), const(di), const(di), pl.BlockSpec(e2.shape, lambda b, c: (0, 0))],
        out_specs=blk(di),
        scratch_shapes=[pltpu.VMEM((SSM_GROUPS, D_STATE, gw), F32),
                        pltpu.VMEM((3 * SSD_CHUNK + 16, di), F32)],
        compiler_params=_params("parallel", "arbitrary"),
        name="ssd",
    )(xs, bm, cm, dt, z, a, d_exp, norm_w, e2)


def _attn_layer(x, g, rope, w_in, w_out, sinks, ckp, ckw1, ckw2, cvp, cvw1, cvw2, bsz, seq):
    k, cmp_in, qt, vt, gates_t = _attn_in(x, g, rope, w_in, seq)
    o_a = _swa(qt, k, vt, sinks, bsz, seq)
    half = CMP_STRIDE * HEAD_DIM
    pos = jnp.stack([ckp, cvp]).reshape(2, 2, half)
    w1 = jnp.stack([ckw1, cvw1]).astype(BF16)
    w2 = jnp.stack([ckw2, cvw2]).astype(BF16)
    cmp_n, cmp_t = _compress(cmp_in, pos, w1, w2, bsz, seq)
    o_b = _nsa(qt, k, vt, cmp_n, cmp_t, gates_t, bsz, seq)
    w = w_out.astype(BF16)
    da = o_a.shape[1]
    return _proj_res(x, [o_a, o_b], [w[:da], w[da:]])


def _mamba_layer(x, g, w_in, w_in_f32, layer, conv_w, conv_b, dt_bias, a_log, d_skip, norm_w, w_out, bsz, seq):
    n_heads = dt_bias.shape[0]
    di = n_heads * SSM_HEAD_DIM
    gn = SSM_GROUPS * D_STATE
    padh = LANES - n_heads
    wdt = jnp.pad(w_in_f32[layer][:, 2 * di + 2 * gn:], ((0, 0), (0, padh))).astype(BF16)
    z, xs, bm, cm, dt = _mamba_in(x, g, w_in, layer, wdt, conv_w, conv_b, jnp.pad(dt_bias, (0, padh)), di, seq)
    a = jnp.pad(-jnp.exp(a_log), (0, padh)).reshape(1, LANES)
    d_exp = jnp.repeat(d_skip, SSM_HEAD_DIM).reshape(1, di)
    yn = _ssd(xs, bm, cm, dt, z, a, d_exp, norm_w.reshape(1, di), bsz, seq)
    return _proj_res(x, [yn], [w_out.astype(BF16)])


def kernel(x, norm_gains, final_norm, ffn_w_gate, ffn_w_up, ffn_w_down, attn_w_in, attn_w_out, attn_sinks, cmp_k_pos, cmp_k_w1, cmp_k_w2, cmp_v_pos, cmp_v_w1, cmp_v_w2, ssm_w_in, ssm_conv_w, ssm_conv_b, ssm_dt_bias, ssm_a_log, ssm_d, ssm_norm, ssm_w_out):
    bsz, seq, d = x.shape
    depth = norm_gains.shape[0]
    rope = _rope_tables(seq)
    wg, wu, wd = ffn_w_gate, ffn_w_up, ffn_w_down
    w_ssm = ssm_w_in.astype(BF16)
    x = x.reshape(bsz * seq, d)
    for i in range(depth):
        g = norm_gains[i]
        x = _ffn(x, g[0], wg, wu, wd, i, 0)
        j = i // 2
        if i % 2 == 0:
            x = _attn_layer(x, g[1], rope, attn_w_in[j], attn_w_out[j], attn_sinks[j],
                            cmp_k_pos[j], cmp_k_w1[j], cmp_k_w2[j],
                            cmp_v_pos[j], cmp_v_w1[j], cmp_v_w2[j], bsz, seq)
        else:
            x = _mamba_layer(x, g[1], w_ssm, ssm_w_in, j, ssm_conv_w[j], ssm_conv_b[j], ssm_dt_bias[j],
                             ssm_a_log[j], ssm_d[j], ssm_norm[j], ssm_w_out[j], bsz, seq)
        x = _ffn(x, g[2], wg, wu, wd, i, 1, final_gain=final_norm if i == depth - 1 else None)
    return x.reshape(bsz, seq, d)
```
